```python
import math
import jax, jax.numpy as jnp
from jax import lax
import numpy as np

D_MODEL = 1024
BATCH = 16
SEQ = 4096
DEPTH = 1

CHUNK = 64
D_MIX = D_MODEL
POOL_WIDTH = D_MIX // 2
POOL_WINDOWS = (2, 4, 8, 16)
POOL_GROUPS = len(POOL_WINDOWS)
POOL_GW = POOL_WIDTH // POOL_GROUPS
HGRN_WIDTH = D_MIX - POOL_WIDTH
HGRN_HEADS = 4
HGRN_DK = HGRN_WIDTH // HGRN_HEADS
HGRN_DV = HGRN_WIDTH // HGRN_HEADS
IN_COLS = POOL_WIDTH + 2 * HGRN_HEADS * HGRN_DK + 2 * HGRN_HEADS * HGRN_DV
N_MEM = 256
XATTN_HEADS = 4
XATTN_HD = D_MODEL // XATTN_HEADS
D_FF = 4 * D_MODEL
EPS = 1e-6

kernel_name = "hybrid_pool_hgrn2_xattn_block"


def rmsnorm(x, gain):
    xf = x.astype(jnp.float32)
    y = xf * lax.rsqrt(jnp.mean(xf * xf, axis=-1, keepdims=True) + EPS)
    return (y * gain.astype(jnp.float32)).astype(x.dtype)


def pool_mixer(u, w_grp, scale):
    B, S, P = u.shape
    uf = u.astype(jnp.float32)
    cs = jnp.concatenate([jnp.zeros((B, 1, P), jnp.float32), jnp.cumsum(uf, axis=1)], axis=1)
    t = jnp.arange(S)
    outs = []
    for g, w in enumerate(POOL_WINDOWS):
        sl = slice(g * POOL_GW, (g + 1) * POOL_GW)
        lo = jnp.maximum(t + 1 - w, 0)
        win_sum = cs[:, 1:, sl] - cs[:, lo, sl]
        cnt = jnp.minimum(t + 1, w).astype(jnp.float32)[None, :, None]
        outs.append(win_sum / cnt - uf[..., sl])
    y = jnp.stack(outs, axis=2).astype(u.dtype)
    y = jnp.einsum('bsgc,gcd->bsgd', y, w_grp).reshape(B, S, P)
    return y * scale


def hgrn2_chunkwise(q, log_f, k, v):
    B, S, H, DK = q.shape
    DV = v.shape[-1]
    n = S // CHUNK

    def to_chunks(a):
        return a.reshape(B, n, CHUNK, H, a.shape[-1]).transpose(0, 1, 3, 2, 4)

    q, log_f, k, v = map(to_chunks, (q, log_f, k, v))
    G = jnp.cumsum(log_f, axis=3)
    G_last = G[:, :, :, -1:]
    G_mid = G[:, :, :, CHUNK // 2 - 1:CHUNK // 2]

    q_rel = q * jnp.exp(G - G_mid)
    k_rel = k * jnp.exp(G_mid - G)
    scores = jnp.einsum('bnhtk,bnhsk->bnhts', q_rel, k_rel)
    causal = jnp.tril(jnp.ones((CHUNK, CHUNK), dtype=bool))
    scores = jnp.where(causal, scores, jnp.zeros_like(scores))
    o_intra = jnp.einsum('bnhts,bnhsv->bnhtv', scores, v)

    k_end = k * jnp.exp(G_last - G)
    dS = jnp.einsum('bnhsk,bnhsv->bnhkv', k_end, v)
    decay = jnp.exp(G_last[:, :, :, 0, :])

    def step(state, inp):
        dS_c, d_c = inp
        return d_c[..., None] * state + dS_c, state

    S0 = jnp.zeros((B, H, DK, DV), q.dtype)
    _, S_prev = lax.scan(step, S0, (dS.transpose(1, 0, 2, 3, 4), decay.transpose(1, 0, 2, 3)))
    S_prev = S_prev.transpose(1, 0, 2, 3, 4)
    o_inter = jnp.einsum('bnhtk,bnhkv->bnhtv', q * jnp.exp(G), S_prev)
    o = o_intra + o_inter
    return o.transpose(0, 1, 3, 2, 4).reshape(B, S, H, DV)


def hgrn2_mixer(z, lb_theta, layer, o_norm):
    B, S, _ = z.shape
    hk = HGRN_HEADS * HGRN_DK
    hv = HGRN_HEADS * HGRN_DV
    zq, zf, zi, zg = jnp.split(z, [hk, 2 * hk, 2 * hk + hv], axis=-1)
    p = jax.nn.softmax(lb_theta.astype(jnp.float32), axis=0)
    lb = jnp.cumsum(p, axis=0)[layer]
    f = lb + (1.0 - lb) * jax.nn.sigmoid(zf.astype(jnp.float32))
    log_f = jnp.log(f).astype(z.dtype)
    k = (1.0 - f).astype(z.dtype)
    q = jax.nn.silu(zq)
    shp = (B, S, HGRN_HEADS, -1)
    o = hgrn2_chunkwise(q.reshape(shp), log_f.reshape(shp), k.reshape(shp), zi.reshape(shp))
    o = rmsnorm(o, o_norm.reshape(HGRN_HEADS, HGRN_DV))
    return o.reshape(B, S, hv) * jax.nn.silu(zg)


def cross_attention(h, mem, wq, wkv, wo):
    B, S, _ = h.shape
    q = (h @ wq).reshape(B, S, XATTN_HEADS, XATTN_HD)
    kv = mem @ wkv
    k, v = jnp.split(kv, 2, axis=-1)
    k = k.reshape(B, N_MEM, XATTN_HEADS, XATTN_HD)
    v = v.reshape(B, N_MEM, XATTN_HEADS, XATTN_HD)
    s = jnp.einsum('bshd,bmhd->bhsm', q, k).astype(jnp.float32) / math.sqrt(XATTN_HD)
    p = jax.nn.softmax(s, axis=-1).astype(h.dtype)
    o = jnp.einsum('bhsm,bmhd->bshd', p, v).reshape(B, S, D_MODEL)
    return o @ wo


def setup_inputs(seed: int = 0) -> dict:
    key = jax.random.key(seed)
    ks = jax.random.split(key, 24)
    n = jax.random.normal
    L = DEPTH
    return {
        "x": n(ks[0], (BATCH, SEQ, D_MODEL), jnp.float32),
        "mem": n(ks[1], (BATCH, N_MEM, D_MODEL), jnp.float32),
        "norm_mix": 1.0 + 0.02 * n(ks[2], (L, D_MODEL), jnp.float32),
        "w_in": n(ks[3], (L, D_MODEL, IN_COLS), jnp.float32) * D_MODEL ** -0.5,
        "pool_w": n(ks[4], (L, POOL_GROUPS, POOL_GW, POOL_GW), jnp.float32) * POOL_GW ** -0.5,
        "pool_scale": 1.0 + 0.02 * n(ks[5], (L, POOL_WIDTH), jnp.float32),
        "lb_theta": 0.1 * n(ks[6], (L + 1, HGRN_HEADS * HGRN_DK), jnp.float32),
        "hgrn_norm": 1.0 + 0.02 * n(ks[7], (L, HGRN_HEADS * HGRN_DV), jnp.float32),
        "w_out": n(ks[8], (L, D_MIX, D_MODEL), jnp.float32) * D_MIX ** -0.5,
        "norm_xq": 1.0 + 0.02 * n(ks[9], (L, D_MODEL), jnp.float32),
        "norm_mem": 1.0 + 0.02 * n(ks[10], (L, D_MODEL), jnp.float32),
        "xw_q": n(ks[11], (L, D_MODEL, D_MODEL), jnp.float32) * D_MODEL ** -0.5,
        "xw_kv": n(ks[12], (L, D_MODEL, 2 * D_MODEL), jnp.float32) * D_MODEL ** -0.5,
        "xw_o": n(ks[13], (L, D_MODEL, D_MODEL), jnp.float32) * D_MODEL ** -0.5,
        "norm_mlp": 1.0 + 0.02 * n(ks[14], (L, D_MODEL), jnp.float32),
        "w_up": n(ks[15], (L, D_MODEL, D_FF), jnp.float32) * D_MODEL ** -0.5,
        "w_down": n(ks[16], (L, D_FF, D_MODEL), jnp.float32) * D_FF ** -0.5,
        "norm_final": 1.0 + 0.02 * n(ks[17], (D_MODEL,), jnp.float32),
    }


def reference(x, mem, norm_mix, w_in, pool_w, pool_scale, lb_theta, hgrn_norm, w_out,
              norm_xq, norm_mem, xw_q, xw_kv, xw_o, norm_mlp, w_up, w_down, norm_final):
    h = x
    for l in range(DEPTH):
        u = rmsnorm(h, norm_mix[l]) @ w_in[l]
        u_pool, u_hgrn = u[..., :POOL_WIDTH], u[..., POOL_WIDTH:]
        y_pool = pool_mixer(u_pool, pool_w[l], pool_scale[l])
        y_hgrn = hgrn2_mixer(u_hgrn, lb_theta, l, hgrn_norm[l])
        h = h + jnp.concatenate([y_pool, y_hgrn], axis=-1) @ w_out[l]
        h = h + cross_attention(rmsnorm(h, norm_xq[l]), rmsnorm(mem, norm_mem[l]),
                                xw_q[l], xw_kv[l], xw_o[l])
        a = jax.nn.relu(rmsnorm(h, norm_mlp[l]) @ w_up[l])
        h = h + (a * a) @ w_down[l]
    return rmsnorm(h, norm_final)
```

```python
import functools
import math

import jax
import jax.numpy as jnp
from jax import lax
from jax.experimental import pallas as pl
from jax.experimental.pallas import tpu as pltpu

D_MODEL = 1024
CHUNK = 64
POOL_WIDTH = 512
POOL_WINDOWS = (2, 4, 8, 16)
POOL_GW = POOL_WIDTH // len(POOL_WINDOWS)
POOL_HIST = 16
HGRN_HEADS = 4
HGRN_DK = 128
HGRN_W = HGRN_HEADS * HGRN_DK
IN_COLS = POOL_WIDTH + 4 * HGRN_W
XATTN_HEADS = 4
XATTN_HD = D_MODEL // XATTN_HEADS
D_FF = 4 * D_MODEL
EPS = 1e-6

MIX_TILE = 256
FFN_TILE = 512
FF_CHUNK = 1024
VMEM_LIMIT = 56 * 1024 * 1024

BF = jnp.bfloat16
F32 = jnp.float32


def _dot(a, b):
    return jnp.dot(a, b, preferred_element_type=F32)


def _dot_nt(a, b):
    return lax.dot_general(a, b, (((1,), (1,)), ((), ())), preferred_element_type=F32)


def _dot_tn(a, b):
    return lax.dot_general(a, b, (((0,), (0,)), ((), ())), preferred_element_type=F32)


def _rms(x, gain_row):
    ms = jnp.mean(x * x, axis=-1, keepdims=True)
    return x * lax.rsqrt(ms + EPS) * gain_row


def _sigmoid(x):
    return 1.0 / (1.0 + jnp.exp(-x))


def _kv_kernel(mem_ref, nmem_ref, wkv_ref, k_ref, v_ref):
    mn = _rms(mem_ref[0], nmem_ref[...]).astype(BF)
    k_ref[0] = (_dot(mn, wkv_ref[:, :D_MODEL]) * (1.0 / math.sqrt(XATTN_HD))).astype(BF)
    v_ref[0] = _dot(mn, wkv_ref[:, D_MODEL:]).astype(BF)


def _const_spec(shape):
    return pl.BlockSpec(shape, lambda *_: (0,) * len(shape), pipeline_mode=pl.Buffered(1))


def _kv_call(mem, norm_mem, wkv):
    B, M, D = mem.shape
    return pl.pallas_call(
        _kv_kernel,
        grid=(B,),
        in_specs=[
            pl.BlockSpec((1, M, D), lambda b: (b, 0, 0)),
            _const_spec((1, D)),
            _const_spec((D, 2 * D)),
        ],
        out_specs=[
            pl.BlockSpec((1, M, D), lambda b: (b, 0, 0)),
            pl.BlockSpec((1, M, D), lambda b: (b, 0, 0)),
        ],
        out_shape=[jax.ShapeDtypeStruct((B, M, D), BF)] * 2,
        compiler_params=pltpu.CompilerParams(
            dimension_semantics=("arbitrary",), vmem_limit_bytes=VMEM_LIMIT),
        name="xattn_kv",
    )(mem, norm_mem, wkv)


def _mixer_kernel(x_ref, nmix_ref, win_ref, poolw_ref, pscale_ref, lbt_ref, hnorm_ref, wout_ref,
                  o_ref, u_ref, ext_ref, st_ref, y_ref, *, layer):
    T = x_ref.shape[1]
    j = pl.program_id(1)

    @pl.when(j == 0)
    def _():
        st_ref[...] = jnp.zeros_like(st_ref)
        ext_ref[0:POOL_HIST, :] = jnp.zeros((POOL_HIST, POOL_WIDTH), F32)

    x = x_ref[0]
    xn = _rms(x, nmix_ref[...]).astype(BF)
    for c in range(0, IN_COLS, 512):
        u_ref[:, c:c + 512] = _dot(xn, win_ref[:, c:c + 512])

    ext_ref[POOL_HIST:POOL_HIST + T, :] = u_ref[:, 0:POOL_WIDTH]
    t_glob = j * T + lax.broadcasted_iota(jnp.int32, (T, POOL_GW), 0)
    for g, w in enumerate(POOL_WINDOWS):
        cols = slice(g * POOL_GW, (g + 1) * POOL_GW)
        e = ext_ref[:, cols]
        s = e
        sh = 1
        while sh < w:
            s = s + pltpu.roll(s, sh, axis=0)
            sh *= 2
        cnt = jnp.minimum(t_glob + 1, w).astype(F32)
        yp = s[POOL_HIST:, :] / cnt - e[POOL_HIST:, :]
        yg = _dot(yp.astype(BF), poolw_ref[g]) * pscale_ref[:, cols]
        y_ref[:, cols] = yg.astype(BF)
    ext_ref[0:POOL_HIST, :] = ext_ref[T:T + POOL_HIST, :]

    th = lbt_ref[...]
    ex = jnp.exp(th - jnp.max(th, axis=0, keepdims=True))
    p = ex / jnp.sum(ex, axis=0, keepdims=True)
    lb = jnp.sum(p[0:layer + 1, :], axis=0, keepdims=True)

    row = lax.broadcasted_iota(jnp.int32, (CHUNK, CHUNK), 0)
    col = lax.broadcasted_iota(jnp.int32, (CHUNK, CHUNK), 1)
    causal = row >= col
    tri = jnp.where(causal, 1.0, 0.0).astype(BF)

    def chunk_body(c, carry):
        rows = pl.ds(pl.multiple_of(c * CHUNK, CHUNK), CHUNK)
        zq = u_ref[rows, POOL_WIDTH:POOL_WIDTH + HGRN_W]
        zf = u_ref[rows, POOL_WIDTH + HGRN_W:POOL_WIDTH + 2 * HGRN_W]
        zi = u_ref[rows, POOL_WIDTH + 2 * HGRN_W:POOL_WIDTH + 3 * HGRN_W]
        zg = u_ref[rows, POOL_WIDTH + 3 * HGRN_W:POOL_WIDTH + 4 * HGRN_W]
        f = lb + (1.0 - lb) * _sigmoid(zf)
        logf = jnp.log(f)
        kk = 1.0 - f
        q = zq * _sigmoid(zq)
        gate = zg * _sigmoid(zg)
        hi = logf.astype(BF)
        r1 = logf - hi.astype(F32)
        mid = r1.astype(BF)
        lo = (r1 - mid.astype(F32)).astype(BF)
        G = _dot(tri, hi) + _dot(tri, mid) + _dot(tri, lo)
        for h in range(HGRN_HEADS):
            hs = slice(h * HGRN_DK, (h + 1) * HGRN_DK)
            Gh = G[:, hs]
            g_mid = Gh[CHUNK // 2 - 1:CHUNK // 2, :]
            g_last = Gh[CHUNK - 1:CHUNK, :]
            qh, kh = q[:, hs], kk[:, hs]
            vb = zi[:, hs].astype(BF)
            q_rel = (qh * jnp.exp(Gh - g_mid)).astype(BF)
            k_rel = (kh * jnp.exp(g_mid - Gh)).astype(BF)
            sc = jnp.where(causal, _dot_nt(q_rel, k_rel), 0.0)
            o_intra = _dot(sc.astype(BF), vb)
            k_end = (kh * jnp.exp(g_last - Gh)).astype(BF)
            d_st = _dot_tn(vb, k_end)
            st = st_ref[h]
            o_inter = _dot_nt((qh * jnp.exp(Gh)).astype(BF), st.astype(BF))
            st_ref[h] = st * jnp.exp(g_last) + d_st
            o = o_intra + o_inter
            on = o * lax.rsqrt(jnp.mean(o * o, axis=-1, keepdims=True) + EPS) * hnorm_ref[:, hs]
            y_ref[rows, POOL_WIDTH + h * HGRN_DK:POOL_WIDTH + (h + 1) * HGRN_DK] = (
                on * gate[:, hs]).astype(BF)
        return carry

    lax.fori_loop(0, T // CHUNK, chunk_body, 0)

    o_ref[0] = x + _dot(y_ref[...], wout_ref[...])


def _mixer_call(x, norm_mix, w_in, pool_w, pool_scale, lb_theta, hgrn_norm, w_out, layer):
    B, S, D = x.shape
    T = MIX_TILE
    return pl.pallas_call(
        functools.partial(_mixer_kernel, layer=layer),
        grid=(B, S // T),
        in_specs=[
            pl.BlockSpec((1, T, D), lambda b, j: (b, j, 0)),
            _const_spec((1, D)),
            _const_spec((D, IN_COLS)),
            _const_spec(pool_w.shape),
            _const_spec((1, POOL_WIDTH)),
            _const_spec(lb_theta.shape),
            _const_spec((1, HGRN_W)),
            _const_spec((D, D)),
        ],
        out_specs=pl.BlockSpec((1, T, D), lambda b, j: (b, j, 0)),
        out_shape=jax.ShapeDtypeStruct((B, S, D), F32),
        scratch_shapes=[
            pltpu.VMEM((T, IN_COLS), F32),
            pltpu.VMEM((POOL_HIST + T, POOL_WIDTH), F32),
            pltpu.VMEM((HGRN_HEADS, HGRN_DK, HGRN_DK), F32),
            pltpu.VMEM((T, D), BF),
        ],
        compiler_params=pltpu.CompilerParams(
            dimension_semantics=("arbitrary", "arbitrary"), vmem_limit_bytes=VMEM_LIMIT),
        name="token_mixer",
    )(x, norm_mix, w_in, pool_w, pool_scale, lb_theta, hgrn_norm, w_out)


def _attn_mlp_kernel(h_ref, k_ref, v_ref, nxq_ref, wq_ref, wo_ref, nmlp_ref, wup_ref, wdown_ref,
                     nfin_ref, o_ref, att_ref, *, final_norm):
    h = h_ref[0]
    hn = _rms(h, nxq_ref[...]).astype(BF)
    q = _dot(hn, wq_ref[...]).astype(BF)
    for hd in range(XATTN_HEADS):
        cs = slice(hd * XATTN_HD, (hd + 1) * XATTN_HD)
        s = _dot_nt(q[:, cs], k_ref[0, :, cs])
        p = jnp.exp(s - jnp.max(s, axis=-1, keepdims=True))
        p = p * (1.0 / jnp.sum(p, axis=-1, keepdims=True))
        att_ref[:, cs] = _dot(p.astype(BF), v_ref[0, :, cs]).astype(BF)
    h = h + _dot(att_ref[...], wo_ref[...])

    hn = _rms(h, nmlp_ref[...]).astype(BF)
    acc = jnp.zeros_like(h)
    for c in range(0, D_FF, FF_CHUNK):
        a = jnp.maximum(_dot(hn, wup_ref[:, c:c + FF_CHUNK]), 0.0)
        acc = acc + _dot((a * a).astype(BF), wdown_ref[c:c + FF_CHUNK, :])
    h = h + acc
    if final_norm:
        h = _rms(h, nfin_ref[...])
    o_ref[0] = h


def _attn_mlp_call(h, k, v, norm_xq, wq, wo, norm_mlp, w_up, w_down, norm_final, final_norm):
    B, S, D = h.shape
    M = k.shape[1]
    T = FFN_TILE
    return pl.pallas_call(
        functools.partial(_attn_mlp_kernel, final_norm=final_norm),
        grid=(B, S // T),
        in_specs=[
            pl.BlockSpec((1, T, D), lambda b, j: (b, j, 0)),
            pl.BlockSpec((1, M, D), lambda b, j: (b, 0, 0)),
            pl.BlockSpec((1, M, D), lambda b, j: (b, 0, 0)),
            _const_spec((1, D)),
            _const_spec((D, D)),
            _const_spec((D, D)),
            _const_spec((1, D)),
            _const_spec((D, D_FF)),
            _const_spec((D_FF, D)),
            _const_spec((1, D)),
        ],
        out_specs=pl.BlockSpec((1, T, D), lambda b, j: (b, j, 0)),
        out_shape=jax.ShapeDtypeStruct((B, S, D), F32),
        scratch_shapes=[pltpu.VMEM((T, D), BF)],
        compiler_params=pltpu.CompilerParams(
            dimension_semantics=("arbitrary", "arbitrary"), vmem_limit_bytes=VMEM_LIMIT),
        name="xattn_mlp",
    )(h, k, v, norm_xq, wq, wo, norm_mlp, w_up, w_down, norm_final)


def kernel(x, mem, norm_mix, w_in, pool_w, pool_scale, lb_theta, hgrn_norm, w_out, norm_xq, norm_mem,
           xw_q, xw_kv, xw_o, norm_mlp, w_up, w_down, norm_final):
    depth = norm_mix.shape[0]
    h = x
    for l in range(depth):
        k, v = _kv_call(mem, norm_mem[l][None], xw_kv[l].astype(BF))
        h = _mixer_call(h, norm_mix[l][None], w_in[l].astype(BF), pool_w[l].astype(BF),
                        pool_scale[l][None], lb_theta, hgrn_norm[l][None], w_out[l].astype(BF), l)
        h = _attn_mlp_call(h, k, v, norm_xq[l][None], xw_q[l].astype(BF), xw_o[l].astype(BF),
                           norm_mlp[l][None], w_up[l].astype(BF), w_down[l].astype(BF),
                           norm_final[None], final_norm=(l == depth - 1))
    return h
```

```python
import functools
import math

import jax
import jax.numpy as jnp
from jax import lax
from jax.experimental import pallas as pl
from jax.experimental.pallas import tpu as pltpu

D_MODEL = 1024
CHUNK = 64
POOL_WIDTH = 512
POOL_WINDOWS = (2, 4, 8, 16)
POOL_GW = POOL_WIDTH // len(POOL_WINDOWS)
POOL_HIST = 16
HGRN_HEADS = 4
HGRN_DK = 128
HGRN_W = HGRN_HEADS * HGRN_DK
IN_COLS = POOL_WIDTH + 4 * HGRN_W
XATTN_HEADS = 4
XATTN_HD = D_MODEL // XATTN_HEADS
D_FF = 4 * D_MODEL
EPS = 1e-6

MIX_TILE = 256
FFN_TILE = 512
FF_CHUNK = 1024
VMEM_LIMIT = 56 * 1024 * 1024

BF = jnp.bfloat16
F32 = jnp.float32


def _dot(a, b):
    return jnp.dot(a, b, preferred_element_type=F32)


def _dot_nt(a, b):
    return lax.dot_general(a, b, (((1,), (1,)), ((), ())), preferred_element_type=F32)


def _dot_tn(a, b):
    return lax.dot_general(a, b, (((0,), (0,)), ((), ())), preferred_element_type=F32)


def _rms(x, gain_row):
    ms = jnp.mean(x * x, axis=-1, keepdims=True)
    return x * lax.rsqrt(ms + EPS) * gain_row


def _sigmoid(x):
    return 1.0 / (1.0 + jnp.exp(-x))


def _kv_kernel(mem_ref, nmem_ref, wkv_ref, k_ref, v_ref):
    mn = _rms(mem_ref[0], nmem_ref[...]).astype(BF)
    k_ref[0] = (_dot(mn, wkv_ref[:, :D_MODEL]) * (1.0 / math.sqrt(XATTN_HD))).astype(BF)
    v_ref[0] = _dot(mn, wkv_ref[:, D_MODEL:]).astype(BF)


def _const_spec(shape):
    return pl.BlockSpec(shape, lambda *_: (0,) * len(shape), pipeline_mode=pl.Buffered(1))


def _kv_call(mem, norm_mem, wkv):
    B, M, D = mem.shape
    return pl.pallas_call(
        _kv_kernel,
        grid=(B,),
        in_specs=[
            pl.BlockSpec((1, M, D), lambda b: (b, 0, 0)),
            _const_spec((1, D)),
            _const_spec((D, 2 * D)),
        ],
        out_specs=[
            pl.BlockSpec((1, M, D), lambda b: (b, 0, 0)),
            pl.BlockSpec((1, M, D), lambda b: (b, 0, 0)),
        ],
        out_shape=[jax.ShapeDtypeStruct((B, M, D), BF)] * 2,
        compiler_params=pltpu.CompilerParams(
            dimension_semantics=("arbitrary",), vmem_limit_bytes=VMEM_LIMIT),
        name="xattn_kv",
    )(mem, norm_mem, wkv)


def _mixer_kernel(x_ref, nmix_ref, win_ref, poolw_ref, pscale_ref, lbt_ref, hnorm_ref, wout_ref,
                  o_ref, u_ref, ext_ref, st_ref, y_ref, *, layer):
    T = x_ref.shape[1]
    j = pl.program_id(1)

    @pl.when(j == 0)
    def _():
        st_ref[...] = jnp.zeros_like(st_ref)
        ext_ref[0:POOL_HIST, :] = jnp.zeros((POOL_HIST, POOL_WIDTH), F32)

    x = x_ref[0]
    xn = _rms(x, nmix_ref[...]).astype(BF)
    for c in range(0, IN_COLS, 512):
        u_ref[:, c:c + 512] = _dot(xn, win_ref[:, c:c + 512])

    ext_ref[POOL_HIST:POOL_HIST + T, :] = u_ref[:, 0:POOL_WIDTH]
    t_glob = j * T + lax.broadcasted_iota(jnp.int32, (T, POOL_GW), 0)
    for g, w in enumerate(POOL_WINDOWS):
        cols = slice(g * POOL_GW, (g + 1) * POOL_GW)
        e = ext_ref[:, cols]
        s = e
        sh = 1
        while sh < w:
            s = s + pltpu.roll(s, sh, axis=0)
            sh *= 2
        cnt = jnp.minimum(t_glob + 1, w).astype(F32)
        yp = s[POOL_HIST:, :] / cnt - e[POOL_HIST:, :]
        yg = _dot(yp.astype(BF), poolw_ref[g]) * pscale_ref[:, cols]
        y_ref[:, cols] = yg.astype(BF)
    ext_ref[0:POOL_HIST, :] = ext_ref[T:T + POOL_HIST, :]

    th = lbt_ref[...]
    ex = jnp.exp(th - jnp.max(th, axis=0, keepdims=True))
    p = ex / jnp.sum(ex, axis=0, keepdims=True)
    lb = jnp.sum(p[0:layer + 1, :], axis=0, keepdims=True)
    f_mid = 0.5 * (1.0 + lb)
    f_amp = 0.5 * (1.0 - lb)

    row = lax.broadcasted_iota(jnp.int32, (CHUNK, HGRN_HEADS * CHUNK), 0)
    col = lax.broadcasted_iota(jnp.int32, (CHUNK, HGRN_HEADS * CHUNK), 1)
    causal = row >= (col & (CHUNK - 1))
    tri2 = jnp.where(causal[:, :2 * CHUNK], 1.0, 0.0).astype(BF)

    def head_blocks(a, width):
        zero = jnp.zeros((CHUNK, width), a.dtype)
        return jnp.concatenate([
            jnp.concatenate([a[:, h * width:(h + 1) * width] if h == r else zero
                             for h in range(HGRN_HEADS)], axis=1)
            for r in range(HGRN_HEADS)], axis=0)

    st = st_ref[...]
    for c in range(T // CHUNK):
        rows = slice(c * CHUNK, (c + 1) * CHUNK)
        zq = u_ref[rows, POOL_WIDTH:POOL_WIDTH + HGRN_W]
        zf = u_ref[rows, POOL_WIDTH + HGRN_W:POOL_WIDTH + 2 * HGRN_W]
        zi = u_ref[rows, POOL_WIDTH + 2 * HGRN_W:POOL_WIDTH + 3 * HGRN_W]
        zg = u_ref[rows, POOL_WIDTH + 3 * HGRN_W:POOL_WIDTH + 4 * HGRN_W]
        f = f_mid + f_amp * jnp.tanh(0.5 * zf)
        l2f = jnp.log2(f)
        kk = 1.0 - f
        hq = 0.5 * zq
        q = hq + hq * jnp.tanh(hq)
        hg = 0.5 * zg
        gate = hg + hg * jnp.tanh(hg)
        hi = l2f.astype(BF)
        lo = (l2f - hi.astype(F32)).astype(BF)
        G = _dot(tri2, jnp.concatenate([hi, lo], axis=0))
        g_mid = G[CHUNK // 2 - 1:CHUNK // 2, :]
        g_last = G[CHUNK - 1:CHUNK, :]
        e_fwd = jnp.exp2(G - g_mid)
        q_rel = q * e_fwd
        k_rel = kk * (1.0 / e_fwd)
        k_end = (k_rel * jnp.exp2(g_last - g_mid)).astype(BF)
        q_dec = (q_rel * jnp.exp2(g_mid)).astype(BF)
        vb = zi.astype(BF)

        sc = _dot_nt(q_rel.astype(BF), head_blocks(k_rel.astype(BF), HGRN_DK))
        sc = jnp.where(causal, sc, 0.0).astype(BF)
        o = _dot(sc, head_blocks(vb, HGRN_DK))
        stb = st.astype(BF)
        zero_st = jnp.zeros((HGRN_DK, HGRN_DK), BF)
        o_inter = []
        for hp in range(0, HGRN_HEADS, 2):
            s0 = stb[:, hp * HGRN_DK:(hp + 1) * HGRN_DK]
            s1 = stb[:, (hp + 1) * HGRN_DK:(hp + 2) * HGRN_DK]
            w = jnp.concatenate([jnp.concatenate([s0, zero_st], axis=1),
                                 jnp.concatenate([zero_st, s1], axis=1)], axis=0)
            o_inter.append(_dot_nt(q_dec[:, hp * HGRN_DK:(hp + 2) * HGRN_DK], w))
        o = o + jnp.concatenate(o_inter, axis=1)
        v_stack = jnp.concatenate([vb[:, h * HGRN_DK:(h + 1) * HGRN_DK] for h in range(HGRN_HEADS)], axis=0)
        st = st * jnp.exp2(g_last) + _dot_tn(v_stack, head_blocks(k_end, HGRN_DK))

        for h in range(HGRN_HEADS):
            hs = slice(h * HGRN_DK, (h + 1) * HGRN_DK)
            oh = o[:, hs]
            on = oh * lax.rsqrt(jnp.mean(oh * oh, axis=-1, keepdims=True) + EPS) * hnorm_ref[:, hs]
            y_ref[rows, POOL_WIDTH + h * HGRN_DK:POOL_WIDTH + (h + 1) * HGRN_DK] = (
                on * gate[:, hs]).astype(BF)
    st_ref[...] = st

    o_ref[0] = x + _dot(y_ref[...], wout_ref[...])


def _mixer_call(x, norm_mix, w_in, pool_w, pool_scale, lb_theta, hgrn_norm, w_out, layer):
    B, S, D = x.shape
    T = MIX_TILE
    return pl.pallas_call(
        functools.partial(_mixer_kernel, layer=layer),
        grid=(B, S // T),
        in_specs=[
            pl.BlockSpec((1, T, D), lambda b, j: (b, j, 0)),
            _const_spec((1, D)),
            _const_spec((D, IN_COLS)),
            _const_spec(pool_w.shape),
            _const_spec((1, POOL_WIDTH)),
            _const_spec(lb_theta.shape),
            _const_spec((1, HGRN_W)),
            _const_spec((D, D)),
        ],
        out_specs=pl.BlockSpec((1, T, D), lambda b, j: (b, j, 0)),
        out_shape=jax.ShapeDtypeStruct((B, S, D), F32),
        scratch_shapes=[
            pltpu.VMEM((T, IN_COLS), F32),
            pltpu.VMEM((POOL_HIST + T, POOL_WIDTH), F32),
            pltpu.VMEM((HGRN_DK, HGRN_W), F32),
            pltpu.VMEM((T, D), BF),
        ],
        compiler_params=pltpu.CompilerParams(
            dimension_semantics=("arbitrary", "arbitrary"), vmem_limit_bytes=VMEM_LIMIT),
        name="token_mixer",
    )(x, norm_mix, w_in, pool_w, pool_scale, lb_theta, hgrn_norm, w_out)


def _attn_mlp_kernel(h_ref, k_ref, v_ref, nxq_ref, wq_ref, wo_ref, nmlp_ref, wup_ref, wdown_ref,
                     nfin_ref, o_ref, att_ref, *, final_norm):
    h = h_ref[0]
    hn = _rms(h, nxq_ref[...]).astype(BF)
    q = _dot(hn, wq_ref[...]).astype(BF)
    for hd in range(XATTN_HEADS):
        cs = slice(hd * XATTN_HD, (hd + 1) * XATTN_HD)
        s = _dot_nt(q[:, cs], k_ref[0, :, cs])
        p = jnp.exp(s - jnp.max(s, axis=-1, keepdims=True))
        p = p * (1.0 / jnp.sum(p, axis=-1, keepdims=True))
        att_ref[:, cs] = _dot(p.astype(BF), v_ref[0, :, cs]).astype(BF)
    h = h + _dot(att_ref[...], wo_ref[...])

    hn = _rms(h, nmlp_ref[...]).astype(BF)
    acc = jnp.zeros_like(h)
    for c in range(0, D_FF, FF_CHUNK):
        a = jnp.maximum(_dot(hn, wup_ref[:, c:c + FF_CHUNK]), 0.0)
        acc = acc + _dot((a * a).astype(BF), wdown_ref[c:c + FF_CHUNK, :])
    h = h + acc
    if final_norm:
        h = _rms(h, nfin_ref[...])
    o_ref[0] = h


def _attn_mlp_call(h, k, v, norm_xq, wq, wo, norm_mlp, w_up, w_down, norm_final, final_norm):
    B, S, D = h.shape
    M = k.shape[1]
    T = FFN_TILE
    return pl.pallas_call(
        functools.partial(_attn_mlp_kernel, final_norm=final_norm),
        grid=(B, S // T),
        in_specs=[
            pl.BlockSpec((1, T, D), lambda b, j: (b, j, 0)),
            pl.BlockSpec((1, M, D), lambda b, j: (b, 0, 0)),
            pl.BlockSpec((1, M, D), lambda b, j: (b, 0, 0)),
            _const_spec((1, D)),
            _const_spec((D, D)),
            _const_spec((D, D)),
            _const_spec((1, D)),
            _const_spec((D, D_FF)),
            _const_spec((D_FF, D)),
            _const_spec((1, D)),
        ],
        out_specs=pl.BlockSpec((1, T, D), lambda b, j: (b, j, 0)),
        out_shape=jax.ShapeDtypeStruct((B, S, D), F32),
        scratch_shapes=[pltpu.VMEM((T, D), BF)],
        compiler_params=pltpu.CompilerParams(
            dimension_semantics=("arbitrary", "arbitrary"), vmem_limit_bytes=VMEM_LIMIT),
        name="xattn_mlp",
    )(h, k, v, norm_xq, wq, wo, norm_mlp, w_up, w_down, norm_final)


def kernel(x, mem, norm_mix, w_in, pool_w, pool_scale, lb_theta, hgrn_norm, w_out, norm_xq, norm_mem,
           xw_q, xw_kv, xw_o, norm_mlp, w_up, w_down, norm_final):
    depth = norm_mix.shape[0]
    h = x
    for l in range(depth):
        k, v = _kv_call(mem, norm_mem[l][None], xw_kv[l].astype(BF))
        h = _mixer_call(h, norm_mix[l][None], w_in[l].astype(BF), pool_w[l].astype(BF),
                        pool_scale[l][None], lb_theta, hgrn_norm[l][None], w_out[l].astype(BF), l)
        h = _attn_mlp_call(h, k, v, norm_xq[l][None], xw_q[l].astype(BF), xw_o[l].astype(BF),
                           norm_mlp[l][None], w_up[l].astype(BF), w_down[l].astype(BF),
                           norm_final[None], final_norm=(l == depth - 1))
    return h
```

```python
import functools
import math

import jax
import jax.numpy as jnp
from jax import lax
from jax.experimental import pallas as pl
from jax.experimental.pallas import tpu as pltpu

D_MODEL = 1024
CHUNK = 64
POOL_WIDTH = 512
POOL_WINDOWS = (2, 4, 8, 16)
POOL_GW = POOL_WIDTH // len(POOL_WINDOWS)
POOL_HIST = 16
HGRN_HEADS = 4
HGRN_DK = 128
HGRN_W = HGRN_HEADS * HGRN_DK
IN_COLS = POOL_WIDTH + 4 * HGRN_W
XATTN_HEADS = 4
XATTN_HD = D_MODEL // XATTN_HEADS
D_FF = 4 * D_MODEL
EPS = 1e-6

MXU_COLS = 256
MIX_TILE = 256
FFN_TILE = 512
FF_CHUNK = 1024
VMEM_LIMIT = 56 * 1024 * 1024

BF = jnp.bfloat16
F32 = jnp.float32


def _dot(a, b):
    return jnp.dot(a, b, preferred_element_type=F32)


def _dot_nt(a, b):
    return lax.dot_general(a, b, (((1,), (1,)), ((), ())), preferred_element_type=F32)


def _dot_tn(a, b):
    return lax.dot_general(a, b, (((0,), (0,)), ((), ())), preferred_element_type=F32)


def _rms(x, gain_row):
    ms = jnp.mean(x * x, axis=-1, keepdims=True)
    return x * lax.rsqrt(ms + EPS) * gain_row


def _const_spec(shape):
    return pl.BlockSpec(shape, lambda *_: (0,) * len(shape), pipeline_mode=pl.Buffered(1))


def _kv_kernel(mem_ref, nmem_ref, wkv_ref, k_ref, v_ref):
    mn = _rms(mem_ref[0], nmem_ref[...]).astype(BF)
    k_ref[0] = (_dot(mn, wkv_ref[:, :D_MODEL]) * (1.0 / math.sqrt(XATTN_HD))).astype(BF)
    v_ref[0] = _dot(mn, wkv_ref[:, D_MODEL:]).astype(BF)


def _kv_call(mem, norm_mem, wkv):
    B, M, D = mem.shape
    return pl.pallas_call(
        _kv_kernel,
        grid=(B,),
        in_specs=[
            pl.BlockSpec((1, M, D), lambda b: (b, 0, 0)),
            _const_spec((1, D)),
            _const_spec((D, 2 * D)),
        ],
        out_specs=[
            pl.BlockSpec((1, M, D), lambda b: (b, 0, 0)),
            pl.BlockSpec((1, M, D), lambda b: (b, 0, 0)),
        ],
        out_shape=[jax.ShapeDtypeStruct((B, M, D), BF)] * 2,
        compiler_params=pltpu.CompilerParams(
            dimension_semantics=("arbitrary",), vmem_limit_bytes=VMEM_LIMIT),
        name="xattn_kv",
    )(mem, norm_mem, wkv)


def _head_blocks(a, width):
    zero = jnp.zeros((CHUNK, width), a.dtype)
    return jnp.concatenate([
        jnp.concatenate([a[:, h * width:(h + 1) * width] if h == r else zero
                         for h in range(HGRN_HEADS)], axis=1)
        for r in range(HGRN_HEADS)], axis=0)


def _pool_tile(u_ref, y_ref, hist, pos0, poolw_ref, pscale_ref):
    T = u_ref.shape[0]
    t_pos = pos0 + lax.broadcasted_iota(jnp.int32, (T, POOL_GW), 0)
    for g, w in enumerate(POOL_WINDOWS):
        cols = slice(g * POOL_GW, (g + 1) * POOL_GW)
        e = jnp.concatenate([hist[:, cols], u_ref[:, cols]], axis=0)
        s = e
        sh = 1
        while sh < w:
            s = s + pltpu.roll(s, sh, axis=0)
            sh *= 2
        cnt = jnp.minimum(t_pos + 1, w).astype(F32)
        yp = s[POOL_HIST:, :] / cnt - e[POOL_HIST:, :]
        yg = _dot(yp.astype(BF), poolw_ref[g]) * pscale_ref[:, cols]
        y_ref[:, cols] = yg.astype(BF)
    return u_ref[T - POOL_HIST:T, 0:POOL_WIDTH]


def _hgrn_gates(u_ref, r0, f_mid, f_amp):
    rows = slice(r0, r0 + CHUNK)
    zq = u_ref[rows, POOL_WIDTH:POOL_WIDTH + HGRN_W]
    zf = u_ref[rows, POOL_WIDTH + HGRN_W:POOL_WIDTH + 2 * HGRN_W]
    zi = u_ref[rows, POOL_WIDTH + 2 * HGRN_W:POOL_WIDTH + 3 * HGRN_W]
    zg = u_ref[rows, POOL_WIDTH + 3 * HGRN_W:POOL_WIDTH + 4 * HGRN_W]
    f = f_mid + f_amp * jnp.tanh(0.5 * zf)
    l2f = jnp.log2(f)
    hq = 0.5 * zq
    hg = 0.5 * zg
    hi = l2f.astype(BF)
    lo = (l2f - hi.astype(F32)).astype(BF)
    return {"kk": 1.0 - f, "q": hq + hq * jnp.tanh(hq), "gate": hg + hg * jnp.tanh(hg),
            "vb": zi.astype(BF), "l2f_split": jnp.concatenate([hi, lo], axis=0)}


def _hgrn_cumsum(ch, tri2):
    ch["G"] = _dot(tri2, ch.pop("l2f_split"))


def _hgrn_decays(ch):
    G = ch.pop("G")
    g_mid = G[CHUNK // 2 - 1:CHUNK // 2, :]
    g_last = G[CHUNK - 1:CHUNK, :]
    e_fwd = jnp.exp2(G - g_mid)
    q_rel = ch.pop("q") * e_fwd
    k_rel = ch.pop("kk") * (1.0 / e_fwd)
    k_end = (k_rel * jnp.exp2(g_last - g_mid)).astype(BF)
    vb = ch.pop("vb")
    ch["q_dec"] = (q_rel * jnp.exp2(g_mid)).astype(BF)
    ch["q_rel"] = q_rel.astype(BF)
    ch["k_rel_blocks"] = _head_blocks(k_rel.astype(BF), HGRN_DK)
    ch["k_end_blocks"] = _head_blocks(k_end, HGRN_DK)
    ch["v_blocks"] = _head_blocks(vb, HGRN_DK)
    ch["v_stack"] = jnp.concatenate([vb[:, h * HGRN_DK:(h + 1) * HGRN_DK] for h in range(HGRN_HEADS)], axis=0)
    ch["decay"] = jnp.exp2(g_last)


def _hgrn_scores(ch, st):
    ch["sc"] = _dot_nt(ch.pop("q_rel"), ch.pop("k_rel_blocks"))
    ch["d_st"] = _dot_tn(ch.pop("v_stack"), ch.pop("k_end_blocks"))
    stb = st.astype(BF)
    zero_st = jnp.zeros((HGRN_DK, HGRN_DK), BF)
    q_dec = ch.pop("q_dec")
    o_inter = []
    for hp in range(0, HGRN_HEADS, 2):
        s0 = stb[:, hp * HGRN_DK:(hp + 1) * HGRN_DK]
        s1 = stb[:, (hp + 1) * HGRN_DK:(hp + 2) * HGRN_DK]
        w = jnp.concatenate([jnp.concatenate([s0, zero_st], axis=1),
                             jnp.concatenate([zero_st, s1], axis=1)], axis=0)
        o_inter.append(_dot_nt(q_dec[:, hp * HGRN_DK:(hp + 2) * HGRN_DK], w))
    ch["o_inter"] = jnp.concatenate(o_inter, axis=1)


def _hgrn_mask_and_state(ch, st, causal):
    ch["sc"] = jnp.where(causal, ch["sc"], 0.0).astype(BF)
    return st * ch.pop("decay") + ch.pop("d_st")


def _hgrn_intra(ch):
    ch["o"] = _dot(ch.pop("sc"), ch.pop("v_blocks"))


def _hgrn_out(ch, y_ref, r0, hnorm_ref):
    o = ch.pop("o") + ch.pop("o_inter")
    gate = ch.pop("gate")
    for h in range(HGRN_HEADS):
        hs = slice(h * HGRN_DK, (h + 1) * HGRN_DK)
        oh = o[:, hs]
        on = oh * lax.rsqrt(jnp.mean(oh * oh, axis=-1, keepdims=True) + EPS) * hnorm_ref[:, hs]
        y_ref[r0:r0 + CHUNK, POOL_WIDTH + h * HGRN_DK:POOL_WIDTH + (h + 1) * HGRN_DK] = (
            on * gate[:, hs]).astype(BF)


def _mixer_kernel(xp_ref, xn_ref, nmix_ref, win_ref, poolw_ref, pscale_ref, lbt_ref, hnorm_ref, wout_ref,
                  o_ref, ua_ref, ub_ref, ya_ref, yb_ref, st_ref, hist_ref, *, layer, tiles_per_seq):
    T = xn_ref.shape[0]
    cpt = T // CHUNK
    n_chunks = 2 * cpt
    g = pl.program_id(0)

    def project_block(xn, u_ref, blk):
        cols = slice(blk * MXU_COLS, (blk + 1) * MXU_COLS)
        u_ref[:, cols] = _dot(xn, win_ref[:, cols])

    def out_block(rows, y_ref, blk):
        cols = slice(blk * MXU_COLS, (blk + 1) * MXU_COLS)
        o_ref[rows, cols] = xp_ref[rows, cols] + _dot(y_ref[...], wout_ref[:, cols])

    n_in_blocks = IN_COLS // MXU_COLS
    n_out_blocks = D_MODEL // MXU_COLS

    @pl.when(g == 0)
    def _():
        xn0 = _rms(xp_ref[0:T, :], nmix_ref[...]).astype(BF)
        for blk in range(n_in_blocks):
            project_block(xn0, ua_ref, blk)
        st_ref[...] = jnp.zeros_like(st_ref)
        hist_ref[...] = jnp.zeros_like(hist_ref)

    th = lbt_ref[...]
    ex = jnp.exp(th - jnp.max(th, axis=0, keepdims=True))
    p = ex / jnp.sum(ex, axis=0, keepdims=True)
    lb = jnp.sum(p[0:layer + 1, :], axis=0, keepdims=True)
    f_mid = 0.5 * (1.0 + lb)
    f_amp = 0.5 * (1.0 - lb)
    row = lax.broadcasted_iota(jnp.int32, (CHUNK, HGRN_HEADS * CHUNK), 0)
    col = lax.broadcasted_iota(jnp.int32, (CHUNK, HGRN_HEADS * CHUNK), 1)
    causal = row >= (col & (CHUNK - 1))
    tri2 = jnp.where(causal[:, :2 * CHUNK], 1.0, 0.0).astype(BF)

    j0 = lax.rem(2 * g, tiles_per_seq)
    st = jnp.where(j0 == 0, 0.0, st_ref[...])
    hist = jnp.where(j0 == 0, 0.0, hist_ref[...])

    def chunk_src(c):
        return (ua_ref, ya_ref, c * CHUNK) if c < cpt else (ub_ref, yb_ref, (c - cpt) * CHUNK)

    early = list(range(-1, cpt - 1))
    late = list(range(cpt - 1, n_chunks + 2))
    proj_b_slots = {s: [] for s in early}
    for blk in range(n_in_blocks):
        proj_b_slots[early[blk * len(early) // n_in_blocks]].append(blk)
    proj_a_slots = {s: [] for s in late}
    for blk in range(n_in_blocks):
        proj_a_slots[late[blk * len(late) // n_in_blocks]].append(blk)
    out_a_slots = {cpt + 2 + blk: blk for blk in range(n_out_blocks)}

    xn_b = _rms(xp_ref[T:2 * T, :], nmix_ref[...]).astype(BF)
    xn_a = None
    chunks = [None] * n_chunks
    for slot in range(-1, n_chunks + 2):
        if 0 <= slot < n_chunks:
            _hgrn_cumsum(chunks[slot], tri2)
        if 0 <= slot - 1 < n_chunks:
            _hgrn_scores(chunks[slot - 1], st)
        if 0 <= slot - 2 < n_chunks:
            _hgrn_intra(chunks[slot - 2])
        for blk in proj_b_slots.get(slot, ()):
            project_block(xn_b, ub_ref, blk)
        for blk in proj_a_slots.get(slot, ()):
            project_block(xn_a, ua_ref, blk)
        if slot in out_a_slots:
            out_block(slice(0, T), ya_ref, out_a_slots[slot])
        if slot == -1:
            hist = _pool_tile(ua_ref, ya_ref, hist, j0 * T, poolw_ref, pscale_ref)
        if slot == cpt - 1:
            hist = _pool_tile(ub_ref, yb_ref, hist, (j0 + 1) * T, poolw_ref, pscale_ref)
        if slot == cpt - 2:
            xn_a = _rms(xn_ref[...], nmix_ref[...]).astype(BF)
        if 0 <= slot + 1 < n_chunks:
            u_ref, _, r0 = chunk_src(slot + 1)
            chunks[slot + 1] = _hgrn_gates(u_ref, r0, f_mid, f_amp)
        if 0 <= slot < n_chunks:
            _hgrn_decays(chunks[slot])
        if 0 <= slot - 1 < n_chunks:
            st = _hgrn_mask_and_state(chunks[slot - 1], st, causal)
        if 0 <= slot - 2 < n_chunks:
            _, y_ref, r0 = chunk_src(slot - 2)
            _hgrn_out(chunks[slot - 2], y_ref, r0, hnorm_ref)

    for blk in range(n_out_blocks):
        out_block(slice(T, 2 * T), yb_ref, blk)
    st_ref[...] = st
    hist_ref[...] = hist


def _mixer_call(x, norm_mix, w_in, pool_w, pool_scale, lb_theta, hgrn_norm, w_out, layer):
    B, S, D = x.shape
    T = MIX_TILE
    tiles_per_seq = S // T
    assert S % (2 * T) == 0
    n_tiles = B * tiles_per_seq
    x2 = x.reshape(B * S, D)
    out = pl.pallas_call(
        functools.partial(_mixer_kernel, layer=layer, tiles_per_seq=tiles_per_seq),
        grid=(n_tiles // 2,),
        in_specs=[
            pl.BlockSpec((2 * T, D), lambda g: (g, 0)),
            pl.BlockSpec((T, D), lambda g: (jnp.minimum(2 * g + 2, n_tiles - 1), 0)),
            _const_spec((1, D)),
            _const_spec((D, IN_COLS)),
            _const_spec(pool_w.shape),
            _const_spec((1, POOL_WIDTH)),
            _const_spec(lb_theta.shape),
            _const_spec((1, HGRN_W)),
            _const_spec((D, D)),
        ],
        out_specs=pl.BlockSpec((2 * T, D), lambda g: (g, 0)),
        out_shape=jax.ShapeDtypeStruct((B * S, D), F32),
        scratch_shapes=[
            pltpu.VMEM((T, IN_COLS), F32),
            pltpu.VMEM((T, IN_COLS), F32),
            pltpu.VMEM((T, D), BF),
            pltpu.VMEM((T, D), BF),
            pltpu.VMEM((HGRN_DK, HGRN_W), F32),
            pltpu.VMEM((POOL_HIST, POOL_WIDTH), F32),
        ],
        compiler_params=pltpu.CompilerParams(
            dimension_semantics=("arbitrary",), vmem_limit_bytes=VMEM_LIMIT),
        name="token_mixer",
    )(x2, x2, norm_mix, w_in, pool_w, pool_scale, lb_theta, hgrn_norm, w_out)
    return out.reshape(B, S, D)


def _attn_mlp_kernel(h_ref, k_ref, v_ref, nxq_ref, wq_ref, wo_ref, nmlp_ref, wup_ref, wdown_ref,
                     nfin_ref, o_ref, att_ref, *, final_norm):
    h = h_ref[0]
    hn = _rms(h, nxq_ref[...]).astype(BF)
    q = _dot(hn, wq_ref[...]).astype(BF)
    for hd in range(XATTN_HEADS):
        cs = slice(hd * XATTN_HD, (hd + 1) * XATTN_HD)
        s = _dot_nt(q[:, cs], k_ref[0, :, cs])
        p = jnp.exp(s - jnp.max(s, axis=-1, keepdims=True))
        p = p * (1.0 / jnp.sum(p, axis=-1, keepdims=True))
        att_ref[:, cs] = _dot(p.astype(BF), v_ref[0, :, cs]).astype(BF)
    h = h + _dot(att_ref[...], wo_ref[...])

    hn = _rms(h, nmlp_ref[...]).astype(BF)
    acc = jnp.zeros_like(h)
    for c in range(0, D_FF, FF_CHUNK):
        a = jnp.maximum(_dot(hn, wup_ref[:, c:c + FF_CHUNK]), 0.0)
        acc = acc + _dot((a * a).astype(BF), wdown_ref[c:c + FF_CHUNK, :])
    h = h + acc
    if final_norm:
        h = _rms(h, nfin_ref[...])
    o_ref[0] = h


def _attn_mlp_call(h, k, v, norm_xq, wq, wo, norm_mlp, w_up, w_down, norm_final, final_norm):
    B, S, D = h.shape
    M = k.shape[1]
    T = FFN_TILE
    return pl.pallas_call(
        functools.partial(_attn_mlp_kernel, final_norm=final_norm),
        grid=(B, S // T),
        in_specs=[
            pl.BlockSpec((1, T, D), lambda b, j: (b, j, 0)),
            pl.BlockSpec((1, M, D), lambda b, j: (b, 0, 0)),
            pl.BlockSpec((1, M, D), lambda b, j: (b, 0, 0)),
            _const_spec((1, D)),
            _const_spec((D, D)),
            _const_spec((D, D)),
            _const_spec((1, D)),
            _const_spec((D, D_FF)),
            _const_spec((D_FF, D)),
            _const_spec((1, D)),
        ],
        out_specs=pl.BlockSpec((1, T, D), lambda b, j: (b, j, 0)),
        out_shape=jax.ShapeDtypeStruct((B, S, D), F32),
        scratch_shapes=[pltpu.VMEM((T, D), BF)],
        compiler_params=pltpu.CompilerParams(
            dimension_semantics=("arbitrary", "arbitrary"), vmem_limit_bytes=VMEM_LIMIT),
        name="xattn_mlp",
    )(h, k, v, norm_xq, wq, wo, norm_mlp, w_up, w_down, norm_final)


def kernel(x, mem, norm_mix, w_in, pool_w, pool_scale, lb_theta, hgrn_norm, w_out, norm_xq, norm_mem,
           xw_q, xw_kv, xw_o, norm_mlp, w_up, w_down, norm_final):
    depth = norm_mix.shape[0]
    h = x
    for l in range(depth):
        k, v = _kv_call(mem, norm_mem[l][None], xw_kv[l].astype(BF))
        h = _mixer_call(h, norm_mix[l][None], w_in[l].astype(BF), pool_w[l].astype(BF),
                        pool_scale[l][None], lb_theta, hgrn_norm[l][None], w_out[l].astype(BF), l)
        h = _attn_mlp_call(h, k, v, norm_xq[l][None], xw_q[l].astype(BF), xw_o[l].astype(BF),
                           norm_mlp[l][None], w_up[l].astype(BF), w_down[l].astype(BF),
                           norm_final[None], final_norm=(l == depth - 1))
    return h
```

```python
import functools
import math

import jax
import jax.numpy as jnp
from jax import lax
from jax.experimental import pallas as pl
from jax.experimental.pallas import tpu as pltpu

D_MODEL = 1024
CHUNK = 64
POOL_WIDTH = 512
POOL_WINDOWS = (2, 4, 8, 16)
POOL_GW = POOL_WIDTH // len(POOL_WINDOWS)
POOL_HIST = 16
HGRN_HEADS = 4
HGRN_DK = 128
HGRN_W = HGRN_HEADS * HGRN_DK
IN_COLS = POOL_WIDTH + 4 * HGRN_W
XATTN_HEADS = 4
XATTN_HD = D_MODEL // XATTN_HEADS
D_FF = 4 * D_MODEL
EPS = 1e-6

MXU_COLS = 256
MIX_TILE = 256
FFN_TILE = 512
FF_CHUNK = 512
VMEM_LIMIT = 56 * 1024 * 1024

BF = jnp.bfloat16
F32 = jnp.float32


def _dot(a, b):
    return jnp.dot(a, b, preferred_element_type=F32)


def _dot_nt(a, b):
    return lax.dot_general(a, b, (((1,), (1,)), ((), ())), preferred_element_type=F32)


def _dot_tn(a, b):
    return lax.dot_general(a, b, (((0,), (0,)), ((), ())), preferred_element_type=F32)


def _rms(x, gain_row):
    ms = jnp.mean(x * x, axis=-1, keepdims=True)
    return x * lax.rsqrt(ms + EPS) * gain_row


def _const_spec(shape):
    return pl.BlockSpec(shape, lambda *_: (0,) * len(shape), pipeline_mode=pl.Buffered(1))


def _kv_kernel(mem_ref, nmem_ref, wkv_ref, k_ref, v_ref):
    mn = _rms(mem_ref[0], nmem_ref[...]).astype(BF)
    k_ref[0] = (_dot(mn, wkv_ref[:, :D_MODEL]) * (1.0 / math.sqrt(XATTN_HD))).astype(BF)
    v_ref[0] = _dot(mn, wkv_ref[:, D_MODEL:]).astype(BF)


def _kv_call(mem, norm_mem, wkv):
    B, M, D = mem.shape
    return pl.pallas_call(
        _kv_kernel,
        grid=(B,),
        in_specs=[
            pl.BlockSpec((1, M, D), lambda b: (b, 0, 0)),
            _const_spec((1, D)),
            _const_spec((D, 2 * D)),
        ],
        out_specs=[
            pl.BlockSpec((1, M, D), lambda b: (b, 0, 0)),
            pl.BlockSpec((1, M, D), lambda b: (b, 0, 0)),
        ],
        out_shape=[jax.ShapeDtypeStruct((B, M, D), BF)] * 2,
        compiler_params=pltpu.CompilerParams(
            dimension_semantics=("arbitrary",), vmem_limit_bytes=VMEM_LIMIT),
        name="xattn_kv",
    )(mem, norm_mem, wkv)


def _head_blocks(a, width):
    zero = jnp.zeros((CHUNK, width), a.dtype)
    return jnp.concatenate([
        jnp.concatenate([a[:, h * width:(h + 1) * width] if h == r else zero
                         for h in range(HGRN_HEADS)], axis=1)
        for r in range(HGRN_HEADS)], axis=0)


def _pool_tile(u_ref, y_ref, hist, pos0, poolw_ref, pscale_ref):
    T = u_ref.shape[0]
    t_pos = pos0 + lax.broadcasted_iota(jnp.int32, (T, POOL_GW), 0)
    for g, w in enumerate(POOL_WINDOWS):
        cols = slice(g * POOL_GW, (g + 1) * POOL_GW)
        e = jnp.concatenate([hist[:, cols], u_ref[:, cols]], axis=0)
        s = e
        sh = 1
        while sh < w:
            s = s + pltpu.roll(s, sh, axis=0)
            sh *= 2
        cnt = jnp.minimum(t_pos + 1, w).astype(F32)
        yp = s[POOL_HIST:, :] / cnt - e[POOL_HIST:, :]
        yg = _dot(yp.astype(BF), poolw_ref[g]) * pscale_ref[:, cols]
        y_ref[:, cols] = yg.astype(BF)
    return u_ref[T - POOL_HIST:T, 0:POOL_WIDTH]


def _hgrn_gates(u_ref, r0, f_mid, f_amp):
    rows = slice(r0, r0 + CHUNK)
    zq = u_ref[rows, POOL_WIDTH:POOL_WIDTH + HGRN_W]
    zf = u_ref[rows, POOL_WIDTH + HGRN_W:POOL_WIDTH + 2 * HGRN_W]
    zi = u_ref[rows, POOL_WIDTH + 2 * HGRN_W:POOL_WIDTH + 3 * HGRN_W]
    zg = u_ref[rows, POOL_WIDTH + 3 * HGRN_W:POOL_WIDTH + 4 * HGRN_W]
    f = f_mid + f_amp * jnp.tanh(0.5 * zf)
    l2f = jnp.log2(f)
    hq = 0.5 * zq
    hg = 0.5 * zg
    hi = l2f.astype(BF)
    lo = (l2f - hi.astype(F32)).astype(BF)
    return {"kk": 1.0 - f, "q": hq + hq * jnp.tanh(hq), "gate": hg + hg * jnp.tanh(hg),
            "vb": zi.astype(BF), "l2f_split": jnp.concatenate([hi, lo], axis=0)}


def _hgrn_cumsum(ch, tri2):
    ch["G"] = _dot(tri2, ch.pop("l2f_split"))


def _hgrn_decays(ch):
    G = ch.pop("G")
    g_mid = G[CHUNK // 2 - 1:CHUNK // 2, :]
    g_last = G[CHUNK - 1:CHUNK, :]
    e_fwd = jnp.exp2(G - g_mid)
    q_rel = ch.pop("q") * e_fwd
    k_rel = ch.pop("kk") * (1.0 / e_fwd)
    k_end = (k_rel * jnp.exp2(g_last - g_mid)).astype(BF)
    vb = ch.pop("vb")
    ch["q_dec"] = (q_rel * jnp.exp2(g_mid)).astype(BF)
    ch["q_rel"] = q_rel.astype(BF)
    ch["k_rel_blocks"] = _head_blocks(k_rel.astype(BF), HGRN_DK)
    ch["k_end_blocks"] = _head_blocks(k_end, HGRN_DK)
    ch["v_blocks"] = _head_blocks(vb, HGRN_DK)
    ch["v_stack"] = jnp.concatenate([vb[:, h * HGRN_DK:(h + 1) * HGRN_DK] for h in range(HGRN_HEADS)], axis=0)
    ch["decay"] = jnp.exp2(g_last)


def _hgrn_scores(ch, st):
    ch["sc"] = _dot_nt(ch.pop("q_rel"), ch.pop("k_rel_blocks"))
    ch["d_st"] = _dot_tn(ch.pop("v_stack"), ch.pop("k_end_blocks"))
    stb = st.astype(BF)
    zero_st = jnp.zeros((HGRN_DK, HGRN_DK), BF)
    q_dec = ch.pop("q_dec")
    o_inter = []
    for hp in range(0, HGRN_HEADS, 2):
        s0 = stb[:, hp * HGRN_DK:(hp + 1) * HGRN_DK]
        s1 = stb[:, (hp + 1) * HGRN_DK:(hp + 2) * HGRN_DK]
        w = jnp.concatenate([jnp.concatenate([s0, zero_st], axis=1),
                             jnp.concatenate([zero_st, s1], axis=1)], axis=0)
        o_inter.append(_dot_nt(q_dec[:, hp * HGRN_DK:(hp + 2) * HGRN_DK], w))
    ch["o_inter"] = jnp.concatenate(o_inter, axis=1)


def _hgrn_mask_and_state(ch, st, causal):
    ch["sc"] = jnp.where(causal, ch["sc"], 0.0).astype(BF)
    return st * ch.pop("decay") + ch.pop("d_st")


def _hgrn_intra(ch):
    ch["o"] = _dot(ch.pop("sc"), ch.pop("v_blocks"))


def _hgrn_out(ch, y_ref, r0, hnorm_ref):
    o = ch.pop("o") + ch.pop("o_inter")
    gate = ch.pop("gate")
    for h in range(HGRN_HEADS):
        hs = slice(h * HGRN_DK, (h + 1) * HGRN_DK)
        oh = o[:, hs]
        on = oh * lax.rsqrt(jnp.mean(oh * oh, axis=-1, keepdims=True) + EPS) * hnorm_ref[:, hs]
        y_ref[r0:r0 + CHUNK, POOL_WIDTH + h * HGRN_DK:POOL_WIDTH + (h + 1) * HGRN_DK] = (
            on * gate[:, hs]).astype(BF)


def _mixer_kernel(xp_ref, xn_ref, nmix_ref, win_ref, poolw_ref, pscale_ref, lbt_ref, hnorm_ref, wout_ref,
                  o_ref, ua_ref, ub_ref, ya_ref, yb_ref, st_ref, hist_ref, *, layer, tiles_per_seq):
    T = xn_ref.shape[0]
    cpt = T // CHUNK
    n_chunks = 2 * cpt
    g = pl.program_id(0)

    def project_block(xn, u_ref, blk):
        cols = slice(blk * MXU_COLS, (blk + 1) * MXU_COLS)
        u_ref[:, cols] = _dot(xn, win_ref[:, cols])

    def out_block(rows, y_ref, blk):
        cols = slice(blk * MXU_COLS, (blk + 1) * MXU_COLS)
        o_ref[rows, cols] = xp_ref[rows, cols] + _dot(y_ref[...], wout_ref[:, cols])

    n_in_blocks = IN_COLS // MXU_COLS
    n_out_blocks = D_MODEL // MXU_COLS

    @pl.when(g == 0)
    def _():
        xn0 = _rms(xp_ref[0:T, :], nmix_ref[...]).astype(BF)
        for blk in range(n_in_blocks):
            project_block(xn0, ua_ref, blk)
        st_ref[...] = jnp.zeros_like(st_ref)
        hist_ref[...] = jnp.zeros_like(hist_ref)

    th = lbt_ref[...]
    ex = jnp.exp(th - jnp.max(th, axis=0, keepdims=True))
    p = ex / jnp.sum(ex, axis=0, keepdims=True)
    lb = jnp.sum(p[0:layer + 1, :], axis=0, keepdims=True)
    f_mid = 0.5 * (1.0 + lb)
    f_amp = 0.5 * (1.0 - lb)
    row = lax.broadcasted_iota(jnp.int32, (CHUNK, HGRN_HEADS * CHUNK), 0)
    col = lax.broadcasted_iota(jnp.int32, (CHUNK, HGRN_HEADS * CHUNK), 1)
    causal = row >= (col & (CHUNK - 1))
    tri2 = jnp.where(causal[:, :2 * CHUNK], 1.0, 0.0).astype(BF)

    j0 = lax.rem(2 * g, tiles_per_seq)
    st = jnp.where(j0 == 0, 0.0, st_ref[...])
    hist = jnp.where(j0 == 0, 0.0, hist_ref[...])

    def chunk_src(c):
        return (ua_ref, ya_ref, c * CHUNK) if c < cpt else (ub_ref, yb_ref, (c - cpt) * CHUNK)

    early = list(range(-1, cpt - 1))
    late = list(range(cpt - 1, n_chunks + 2))
    proj_b_slots = {s: [] for s in early}
    for blk in range(n_in_blocks):
        proj_b_slots[early[blk * len(early) // n_in_blocks]].append(blk)
    proj_a_slots = {s: [] for s in late}
    for blk in range(n_in_blocks):
        proj_a_slots[late[blk * len(late) // n_in_blocks]].append(blk)
    out_a_slots = {cpt + 2 + blk: blk for blk in range(n_out_blocks)}

    xn_b = _rms(xp_ref[T:2 * T, :], nmix_ref[...]).astype(BF)
    xn_a = None
    chunks = [None] * n_chunks
    for slot in range(-1, n_chunks + 2):
        if 0 <= slot < n_chunks:
            _hgrn_cumsum(chunks[slot], tri2)
        if 0 <= slot - 1 < n_chunks:
            _hgrn_scores(chunks[slot - 1], st)
        if 0 <= slot - 2 < n_chunks:
            _hgrn_intra(chunks[slot - 2])
        for blk in proj_b_slots.get(slot, ()):
            project_block(xn_b, ub_ref, blk)
        for blk in proj_a_slots.get(slot, ()):
            project_block(xn_a, ua_ref, blk)
        if slot in out_a_slots:
            out_block(slice(0, T), ya_ref, out_a_slots[slot])
        if slot == -1:
            hist = _pool_tile(ua_ref, ya_ref, hist, j0 * T, poolw_ref, pscale_ref)
        if slot == cpt - 1:
            hist = _pool_tile(ub_ref, yb_ref, hist, (j0 + 1) * T, poolw_ref, pscale_ref)
        if slot == cpt - 2:
            xn_a = _rms(xn_ref[...], nmix_ref[...]).astype(BF)
        if 0 <= slot + 1 < n_chunks:
            u_ref, _, r0 = chunk_src(slot + 1)
            chunks[slot + 1] = _hgrn_gates(u_ref, r0, f_mid, f_amp)
        if 0 <= slot < n_chunks:
            _hgrn_decays(chunks[slot])
        if 0 <= slot - 1 < n_chunks:
            st = _hgrn_mask_and_state(chunks[slot - 1], st, causal)
        if 0 <= slot - 2 < n_chunks:
            _, y_ref, r0 = chunk_src(slot - 2)
            _hgrn_out(chunks[slot - 2], y_ref, r0, hnorm_ref)

    for blk in range(n_out_blocks):
        out_block(slice(T, 2 * T), yb_ref, blk)
    st_ref[...] = st
    hist_ref[...] = hist


def _mixer_call(x, norm_mix, w_in, pool_w, pool_scale, lb_theta, hgrn_norm, w_out, layer):
    B, S, D = x.shape
    T = MIX_TILE
    tiles_per_seq = S // T
    assert S % (2 * T) == 0
    n_tiles = B * tiles_per_seq
    x2 = x.reshape(B * S, D)
    out = pl.pallas_call(
        functools.partial(_mixer_kernel, layer=layer, tiles_per_seq=tiles_per_seq),
        grid=(n_tiles // 2,),
        in_specs=[
            pl.BlockSpec((2 * T, D), lambda g: (g, 0)),
            pl.BlockSpec((T, D), lambda g: (jnp.minimum(2 * g + 2, n_tiles - 1), 0)),
            _const_spec((1, D)),
            _const_spec((D, IN_COLS)),
            _const_spec(pool_w.shape),
            _const_spec((1, POOL_WIDTH)),
            _const_spec(lb_theta.shape),
            _const_spec((1, HGRN_W)),
            _const_spec((D, D)),
        ],
        out_specs=pl.BlockSpec((2 * T, D), lambda g: (g, 0)),
        out_shape=jax.ShapeDtypeStruct((B * S, D), F32),
        scratch_shapes=[
            pltpu.VMEM((T, IN_COLS), F32),
            pltpu.VMEM((T, IN_COLS), F32),
            pltpu.VMEM((T, D), BF),
            pltpu.VMEM((T, D), BF),
            pltpu.VMEM((HGRN_DK, HGRN_W), F32),
            pltpu.VMEM((POOL_HIST, POOL_WIDTH), F32),
        ],
        compiler_params=pltpu.CompilerParams(
            dimension_semantics=("arbitrary",), vmem_limit_bytes=VMEM_LIMIT),
        name="token_mixer",
    )(x2, x2, norm_mix, w_in, pool_w, pool_scale, lb_theta, hgrn_norm, w_out)
    return out.reshape(B, S, D)


def _attn_mlp_kernel(h_ref, k_ref, v_ref, nxq_ref, wq_ref, wo_ref, nmlp_ref, wup_ref, wdown_ref,
                     nfin_ref, o_ref, h2_ref, hn2_ref, att_ref, *, final_norm):
    s = pl.program_id(0)

    @pl.when(s == 0)
    def _():
        h2_ref[...] = jnp.zeros_like(h2_ref)
        hn2_ref[...] = jnp.zeros_like(hn2_ref)

    n_blk = D_FF // FF_CHUNK
    assert n_blk >= XATTN_HEADS + 4
    T = h_ref.shape[0]
    h = h_ref[...]
    att = {}

    def head_cols(hd):
        return slice(hd * XATTN_HD, (hd + 1) * XATTN_HD)

    def attn_matmuls(i):
        if i == 1:
            att["q"] = _dot(att.pop("hn"), wq_ref[...])
        if 3 <= i < 3 + XATTN_HEADS:
            hd = i - 3
            att["o", hd] = _dot(att.pop(("p", hd)), v_ref[0, :, head_cols(hd)])
        if 2 <= i < 2 + XATTN_HEADS:
            hd = i - 2
            att["s", hd] = _dot_nt(att["qb"][:, head_cols(hd)], k_ref[0, :, head_cols(hd)])
        if i == 3 + XATTN_HEADS:
            for r in range(0, T, T // 2):
                att["proj", r] = _dot(att_ref[r:r + T // 2, :], wo_ref[...])

    def attn_vectors(i):
        if i == 0:
            att["hn"] = _rms(h, nxq_ref[...]).astype(BF)
        if i == 1:
            att["qb"] = att.pop("q").astype(BF)
        if 3 <= i < 3 + XATTN_HEADS:
            hd = i - 3
            att_ref[:, head_cols(hd)] = att.pop(("o", hd)).astype(BF)
        if 2 <= i < 2 + XATTN_HEADS:
            hd = i - 2
            sc = att.pop(("s", hd))
            p = jnp.exp(sc - jnp.max(sc, axis=-1, keepdims=True))
            att["p", hd] = (p * (1.0 / jnp.sum(p, axis=-1, keepdims=True))).astype(BF)
        if i == 3 + XATTN_HEADS:
            for r in range(0, T, T // 2):
                rows = slice(r, r + T // 2)
                h2 = h_ref[rows, :] + att.pop(("proj", r))
                h2_ref[rows, :] = h2
                hn2_ref[rows, :] = _rms(h2, nmlp_ref[...]).astype(BF)

    acc = None
    up = act = None
    for i in range(n_blk):
        attn_matmuls(i)
        if i >= 1:
            down = _dot(act, wdown_ref[(i - 1) * FF_CHUNK:i * FF_CHUNK, :])
            acc = h2_ref[...] + down if i == 1 else acc + down
        up = _dot(hn2_ref[...], wup_ref[:, i * FF_CHUNK:(i + 1) * FF_CHUNK])
        attn_vectors(i)
        a = jnp.maximum(up, 0.0)
        act = (a * a).astype(BF)
    halves = [slice(r, r + T // 2) for r in range(0, T, T // 2)]
    downs = [_dot(act[rows, :], wdown_ref[(n_blk - 1) * FF_CHUNK:, :]) for rows in halves]
    for rows, down in zip(halves, downs):
        out = acc[rows, :] + down
        o_ref[rows, :] = _rms(out, nfin_ref[...]) if final_norm else out


def _attn_mlp_call(h, k, v, norm_xq, wq, wo, norm_mlp, w_up, w_down, norm_final, final_norm):
    B, S, D = h.shape
    M = k.shape[1]
    T = FFN_TILE
    tiles_per_seq = S // T
    n_tiles = B * tiles_per_seq

    def attn_tile(s):
        return jnp.minimum(s, n_tiles - 1)

    out = pl.pallas_call(
        functools.partial(_attn_mlp_kernel, final_norm=final_norm),
        grid=(n_tiles + 1,),
        in_specs=[
            pl.BlockSpec((T, D), lambda s: (attn_tile(s), 0)),
            pl.BlockSpec((1, M, D), lambda s: (attn_tile(s) // tiles_per_seq, 0, 0)),
            pl.BlockSpec((1, M, D), lambda s: (attn_tile(s) // tiles_per_seq, 0, 0)),
            _const_spec((1, D)),
            _const_spec((D, D)),
            _const_spec((D, D)),
            _const_spec((1, D)),
            _const_spec((D, D_FF)),
            _const_spec((D_FF, D)),
            _const_spec((1, D)),
        ],
        out_specs=pl.BlockSpec((T, D), lambda s: (jnp.maximum(s - 1, 0), 0)),
        out_shape=jax.ShapeDtypeStruct((B * S, D), F32),
        scratch_shapes=[
            pltpu.VMEM((T, D), F32),
            pltpu.VMEM((T, D), BF),
            pltpu.VMEM((T, D), BF),
        ],
        compiler_params=pltpu.CompilerParams(
            dimension_semantics=("arbitrary",), vmem_limit_bytes=VMEM_LIMIT),
        name="xattn_mlp",
    )(h.reshape(B * S, D), k, v, norm_xq, wq, wo, norm_mlp, w_up, w_down, norm_final)
    return out.reshape(B, S, D)


def kernel(x, mem, norm_mix, w_in, pool_w, pool_scale, lb_theta, hgrn_norm, w_out, norm_xq, norm_mem,
           xw_q, xw_kv, xw_o, norm_mlp, w_up, w_down, norm_final):
    depth = norm_mix.shape[0]
    h = x
    for l in range(depth):
        k, v = _kv_call(mem, norm_mem[l][None], xw_kv[l].astype(BF))
        h = _mixer_call(h, norm_mix[l][None], w_in[l].astype(BF), pool_w[l].astype(BF),
                        pool_scale[l][None], lb_theta, hgrn_norm[l][None], w_out[l].astype(BF), l)
        h = _attn_mlp_call(h, k, v, norm_xq[l][None], xw_q[l].astype(BF), xw_o[l].astype(BF),
                           norm_mlp[l][None], w_up[l].astype(BF), w_down[l].astype(BF),
                           norm_final[None], final_norm=(l == depth - 1))
    return h
```

```python
import functools
import math

import jax
import jax.numpy as jnp
from jax import lax
from jax.experimental import pallas as pl
from jax.experimental.pallas import tpu as pltpu

D_MODEL = 1024
CHUNK = 64
POOL_WIDTH = 512
POOL_WINDOWS = (2, 4, 8, 16)
POOL_GW = POOL_WIDTH // len(POOL_WINDOWS)
POOL_HIST = 16
HGRN_HEADS = 4
HGRN_DK = 128
HGRN_W = HGRN_HEADS * HGRN_DK
IN_COLS = POOL_WIDTH + 4 * HGRN_W
XATTN_HEADS = 4
XATTN_HD = D_MODEL // XATTN_HEADS
D_FF = 4 * D_MODEL
EPS = 1e-6

MXU_COLS = 256
MIX_TILE = 256
FFN_TILE = 512
FF_CHUNK = 512
VMEM_LIMIT = 56 * 1024 * 1024

BF = jnp.bfloat16
F32 = jnp.float32


def _dot(a, b):
    return jnp.dot(a, b, preferred_element_type=F32)


def _dot_nt(a, b):
    return lax.dot_general(a, b, (((1,), (1,)), ((), ())), preferred_element_type=F32)


def _dot_tn(a, b):
    return lax.dot_general(a, b, (((0,), (0,)), ((), ())), preferred_element_type=F32)


def _rms(x, gain_row):
    ms = jnp.mean(x * x, axis=-1, keepdims=True)
    return x * lax.rsqrt(ms + EPS) * gain_row


def _const_spec(shape):
    return pl.BlockSpec(shape, lambda *_: (0,) * len(shape), pipeline_mode=pl.Buffered(1))


def _kv_kernel(mem_ref, nmem_ref, wkv_ref, k_ref, v_ref):
    mn = _rms(mem_ref[0], nmem_ref[...]).astype(BF)
    k_ref[0] = (_dot(mn, wkv_ref[:, :D_MODEL]) * (1.0 / math.sqrt(XATTN_HD))).astype(BF)
    v_ref[0] = _dot(mn, wkv_ref[:, D_MODEL:]).astype(BF)


def _kv_call(mem, norm_mem, wkv):
    B, M, D = mem.shape
    return pl.pallas_call(
        _kv_kernel,
        grid=(B,),
        in_specs=[
            pl.BlockSpec((1, M, D), lambda b: (b, 0, 0)),
            _const_spec((1, D)),
            _const_spec((D, 2 * D)),
        ],
        out_specs=[
            pl.BlockSpec((1, M, D), lambda b: (b, 0, 0)),
            pl.BlockSpec((1, M, D), lambda b: (b, 0, 0)),
        ],
        out_shape=[jax.ShapeDtypeStruct((B, M, D), BF)] * 2,
        compiler_params=pltpu.CompilerParams(
            dimension_semantics=("arbitrary",), vmem_limit_bytes=VMEM_LIMIT),
        name="xattn_kv",
    )(mem, norm_mem, wkv)


def _head_blocks(a, width):
    zero = jnp.zeros((CHUNK, width), a.dtype)
    return jnp.concatenate([
        jnp.concatenate([a[:, h * width:(h + 1) * width] if h == r else zero
                         for h in range(HGRN_HEADS)], axis=1)
        for r in range(HGRN_HEADS)], axis=0)


def _pool_tile(u_ref, y_ref, hist, pos0, poolw_ref, pscale_ref):
    T = u_ref.shape[0]
    t_pos = pos0 + lax.broadcasted_iota(jnp.int32, (T, POOL_GW), 0)
    for g, w in enumerate(POOL_WINDOWS):
        cols = slice(g * POOL_GW, (g + 1) * POOL_GW)
        e = jnp.concatenate([hist[:, cols], u_ref[:, cols]], axis=0)
        s = e
        sh = 1
        while sh < w:
            s = s + pltpu.roll(s, sh, axis=0)
            sh *= 2
        cnt = jnp.minimum(t_pos + 1, w).astype(F32)
        yp = s[POOL_HIST:, :] / cnt - e[POOL_HIST:, :]
        yg = _dot(yp.astype(BF), poolw_ref[g]) * pscale_ref[:, cols]
        y_ref[:, cols] = yg.astype(BF)
    return u_ref[T - POOL_HIST:T, 0:POOL_WIDTH]


def _hgrn_gates(u_ref, r0, f_mid, f_amp):
    rows = slice(r0, r0 + CHUNK)
    zq = u_ref[rows, POOL_WIDTH:POOL_WIDTH + HGRN_W]
    zf = u_ref[rows, POOL_WIDTH + HGRN_W:POOL_WIDTH + 2 * HGRN_W]
    zi = u_ref[rows, POOL_WIDTH + 2 * HGRN_W:POOL_WIDTH + 3 * HGRN_W]
    zg = u_ref[rows, POOL_WIDTH + 3 * HGRN_W:POOL_WIDTH + 4 * HGRN_W]
    f = f_mid + f_amp * jnp.tanh(0.5 * zf)
    l2f = jnp.log2(f)
    hq = 0.5 * zq
    hg = 0.5 * zg
    hi = l2f.astype(BF)
    lo = (l2f - hi.astype(F32)).astype(BF)
    return {"kk": 1.0 - f, "q": hq + hq * jnp.tanh(hq), "gate": hg + hg * jnp.tanh(hg),
            "vb": zi.astype(BF), "l2f_split": jnp.concatenate([hi, lo], axis=0)}


def _hgrn_cumsum(ch, tri2):
    ch["G"] = _dot(tri2, ch.pop("l2f_split"))


def _hgrn_decays(ch):
    G = ch.pop("G")
    g_mid = G[CHUNK // 2 - 1:CHUNK // 2, :]
    g_last = G[CHUNK - 1:CHUNK, :]
    e_fwd = jnp.exp2(G - g_mid)
    q_rel = ch.pop("q") * e_fwd
    k_rel = ch.pop("kk") * (1.0 / e_fwd)
    k_end = (k_rel * jnp.exp2(g_last - g_mid)).astype(BF)
    vb = ch.pop("vb")
    ch["q_dec"] = (q_rel * jnp.exp2(g_mid)).astype(BF)
    ch["q_rel"] = q_rel.astype(BF)
    ch["k_rel_blocks"] = _head_blocks(k_rel, HGRN_DK).T.astype(BF)
    ch["k_end"] = k_end
    ch["vb"] = vb
    ch["decay"] = jnp.exp2(g_last)


def _pair_blocks(a, hp, width):
    zero = jnp.zeros((a.shape[0], width), a.dtype)
    a0 = a[:, hp * width:(hp + 1) * width]
    a1 = a[:, (hp + 1) * width:(hp + 2) * width]
    return jnp.concatenate([jnp.concatenate([a0, zero], axis=1),
                            jnp.concatenate([zero, a1], axis=1)], axis=0)


def _hgrn_scores(ch):
    ch["sc"] = _dot(ch.pop("q_rel"), ch.pop("k_rel_blocks"))


def _state_operands(st):
    return {hp: _pair_blocks(st, hp, HGRN_DK).T.astype(BF) for hp in range(0, HGRN_HEADS, 2)}


def _hgrn_inter(ch, st_ops, hp):
    pair = slice(hp * HGRN_DK, (hp + 2) * HGRN_DK)
    ch["o_inter", hp] = _dot(ch["q_dec"][:, pair], st_ops[hp])


def _hgrn_state_update(ch, hp):
    vb = ch["vb"]
    v_pair = jnp.concatenate([vb[:, hp * HGRN_DK:(hp + 1) * HGRN_DK],
                              vb[:, (hp + 1) * HGRN_DK:(hp + 2) * HGRN_DK]], axis=0)
    ch["d_st", hp] = _dot_tn(v_pair, _pair_blocks(ch["k_end"], hp, HGRN_DK))


def _hgrn_mask_and_state(ch, st, causal):
    ch["sc"] = jnp.where(causal, ch["sc"], 0.0).astype(BF)
    del ch["q_dec"], ch["k_end"]
    d_st = jnp.concatenate([ch.pop(("d_st", hp)) for hp in range(0, HGRN_HEADS, 2)], axis=1)
    return st * ch.pop("decay") + d_st


def _hgrn_intra(ch, hp):
    ch["o", hp] = _dot(ch["sc"][:, hp * CHUNK:(hp + 2) * CHUNK], _pair_blocks(ch["vb"], hp, HGRN_DK))


def _hgrn_out(ch, y_ref, r0, hnorm_ref):
    del ch["sc"], ch["vb"]
    o = jnp.concatenate([ch.pop(("o", hp)) + ch.pop(("o_inter", hp)) for hp in range(0, HGRN_HEADS, 2)], axis=1)
    gate = ch.pop("gate")
    for h in range(HGRN_HEADS):
        hs = slice(h * HGRN_DK, (h + 1) * HGRN_DK)
        oh = o[:, hs]
        on = oh * lax.rsqrt(jnp.mean(oh * oh, axis=-1, keepdims=True) + EPS) * hnorm_ref[:, hs]
        y_ref[r0:r0 + CHUNK, POOL_WIDTH + h * HGRN_DK:POOL_WIDTH + (h + 1) * HGRN_DK] = (
            on * gate[:, hs]).astype(BF)


def _mixer_kernel(xp_ref, xn_ref, nmix_ref, win_ref, poolw_ref, pscale_ref, lbt_ref, hnorm_ref, wout_ref,
                  o_ref, ua_ref, ub_ref, ya_ref, yb_ref, st_ref, hist_ref, *, layer, tiles_per_seq):
    T = xn_ref.shape[0]
    cpt = T // CHUNK
    n_chunks = 2 * cpt
    g = pl.program_id(0)

    def project_block(xn, u_ref, blk):
        cols = slice(blk * MXU_COLS, (blk + 1) * MXU_COLS)
        u_ref[:, cols] = _dot(xn, win_ref[:, cols])

    def out_block(rows, y_ref, blk):
        cols = slice(blk * MXU_COLS, (blk + 1) * MXU_COLS)
        o_ref[rows, cols] = xp_ref[rows, cols] + _dot(y_ref[...], wout_ref[:, cols])

    n_in_blocks = IN_COLS // MXU_COLS
    n_out_blocks = D_MODEL // MXU_COLS

    @pl.when(g == 0)
    def _():
        xn0 = _rms(xp_ref[0:T, :], nmix_ref[...]).astype(BF)
        for blk in range(n_in_blocks):
            project_block(xn0, ua_ref, blk)
        st_ref[...] = jnp.zeros_like(st_ref)
        hist_ref[...] = jnp.zeros_like(hist_ref)

    th = lbt_ref[...]
    ex = jnp.exp(th - jnp.max(th, axis=0, keepdims=True))
    p = ex / jnp.sum(ex, axis=0, keepdims=True)
    lb = jnp.sum(p[0:layer + 1, :], axis=0, keepdims=True)
    f_mid = 0.5 * (1.0 + lb)
    f_amp = 0.5 * (1.0 - lb)
    row = lax.broadcasted_iota(jnp.int32, (CHUNK, HGRN_HEADS * CHUNK), 0)
    col = lax.broadcasted_iota(jnp.int32, (CHUNK, HGRN_HEADS * CHUNK), 1)
    causal = row >= (col & (CHUNK - 1))
    tri2 = jnp.where(causal[:, :2 * CHUNK], 1.0, 0.0).astype(BF)

    j0 = lax.rem(2 * g, tiles_per_seq)
    st = jnp.where(j0 == 0, 0.0, st_ref[...])
    hist = jnp.where(j0 == 0, 0.0, hist_ref[...])

    def chunk_src(c):
        return (ua_ref, ya_ref, c * CHUNK) if c < cpt else (ub_ref, yb_ref, (c - cpt) * CHUNK)

    early = list(range(-1, cpt - 1))
    late = list(range(cpt - 1, n_chunks + 2))
    proj_b_slots = {s: [] for s in early}
    pairs = [list(range(b, min(b + 2, n_in_blocks))) for b in range(0, n_in_blocks, 2)]
    for i, pair in enumerate(pairs):
        proj_b_slots[early[max(0, i - (len(pairs) - len(early)))]].extend(pair)
    proj_a_slots = {s: [] for s in late}
    out_a_slots = {}
    proj_a_todo = list(range(n_in_blocks))
    out_a_todo = list(range(n_out_blocks))
    for s in late:
        room = 2
        if s >= cpt + 2 and out_a_todo:
            out_a_slots[s] = out_a_todo.pop(0)
            room -= 1
        while room > 0 and proj_a_todo:
            proj_a_slots[s].append(proj_a_todo.pop(0))
            room -= 1
    proj_a_slots[late[-1]].extend(proj_a_todo)
    assert not out_a_todo

    xn_b = _rms(xp_ref[T:2 * T, :], nmix_ref[...]).astype(BF)
    xn_a = None
    st_ops = _state_operands(st)
    chunks = [None] * n_chunks
    for slot in range(-1, n_chunks + 2):
        big = [functools.partial(project_block, xn_b, ub_ref, blk) for blk in proj_b_slots.get(slot, ())]
        big += [functools.partial(project_block, xn_a, ua_ref, blk) for blk in proj_a_slots.get(slot, ())]
        if slot in out_a_slots:
            big.append(functools.partial(out_block, slice(0, T), ya_ref, out_a_slots[slot]))
        head_pairs = list(range(0, HGRN_HEADS, 2))
        small = [[] for _ in range(1 + len(head_pairs))]
        if 0 <= slot - 1 < n_chunks:
            small[0].append(functools.partial(_hgrn_scores, chunks[slot - 1]))
            for i, hp in enumerate(head_pairs):
                small[1 + i] += [functools.partial(_hgrn_inter, chunks[slot - 1], st_ops, hp),
                                 functools.partial(_hgrn_state_update, chunks[slot - 1], hp)]
        if 0 <= slot < n_chunks:
            small[0].append(functools.partial(_hgrn_cumsum, chunks[slot], tri2))
        if 0 <= slot - 2 < n_chunks:
            for i, hp in enumerate(head_pairs):
                small[1 + i].append(functools.partial(_hgrn_intra, chunks[slot - 2], hp))
        for job in small.pop(0):
            job()
        while big or small:
            if big:
                big.pop(0)()
            if small:
                for job in small.pop(0):
                    job()
        if slot == -1:
            hist = _pool_tile(ua_ref, ya_ref, hist, j0 * T, poolw_ref, pscale_ref)
        if slot == 0:
            hist = _pool_tile(ub_ref, yb_ref, hist, (j0 + 1) * T, poolw_ref, pscale_ref)
        if slot == cpt - 2:
            xn_a = _rms(xn_ref[...], nmix_ref[...]).astype(BF)
        if 0 <= slot + 1 < n_chunks:
            u_ref, _, r0 = chunk_src(slot + 1)
            chunks[slot + 1] = _hgrn_gates(u_ref, r0, f_mid, f_amp)
        if 0 <= slot < n_chunks:
            _hgrn_decays(chunks[slot])
        if 0 <= slot - 1 < n_chunks:
            st = _hgrn_mask_and_state(chunks[slot - 1], st, causal)
            if slot < n_chunks:
                st_ops = _state_operands(st)
        if 0 <= slot - 2 < n_chunks:
            _, y_ref, r0 = chunk_src(slot - 2)
            _hgrn_out(chunks[slot - 2], y_ref, r0, hnorm_ref)

    for blk in range(n_out_blocks):
        out_block(slice(T, 2 * T), yb_ref, blk)
    st_ref[...] = st
    hist_ref[...] = hist


def _mixer_call(x, norm_mix, w_in, pool_w, pool_scale, lb_theta, hgrn_norm, w_out, layer):
    B, S, D = x.shape
    T = MIX_TILE
    tiles_per_seq = S // T
    assert S % (2 * T) == 0
    n_tiles = B * tiles_per_seq
    x2 = x.reshape(B * S, D)
    out = pl.pallas_call(
        functools.partial(_mixer_kernel, layer=layer, tiles_per_seq=tiles_per_seq),
        grid=(n_tiles // 2,),
        in_specs=[
            pl.BlockSpec((2 * T, D), lambda g: (g, 0)),
            pl.BlockSpec((T, D), lambda g: (jnp.minimum(2 * g + 2, n_tiles - 1), 0)),
            _const_spec((1, D)),
            _const_spec((D, IN_COLS)),
            _const_spec(pool_w.shape),
            _const_spec((1, POOL_WIDTH)),
            _const_spec(lb_theta.shape),
            _const_spec((1, HGRN_W)),
            _const_spec((D, D)),
        ],
        out_specs=pl.BlockSpec((2 * T, D), lambda g: (g, 0)),
        out_shape=jax.ShapeDtypeStruct((B * S, D), F32),
        scratch_shapes=[
            pltpu.VMEM((T, IN_COLS), F32),
            pltpu.VMEM((T, IN_COLS), F32),
            pltpu.VMEM((T, D), BF),
            pltpu.VMEM((T, D), BF),
            pltpu.VMEM((HGRN_DK, HGRN_W), F32),
            pltpu.VMEM((POOL_HIST, POOL_WIDTH), F32),
        ],
        compiler_params=pltpu.CompilerParams(
            dimension_semantics=("arbitrary",), vmem_limit_bytes=VMEM_LIMIT),
        name="token_mixer",
    )(x2, x2, norm_mix, w_in, pool_w, pool_scale, lb_theta, hgrn_norm, w_out)
    return out.reshape(B, S, D)


def _attn_mlp_kernel(h_ref, k_ref, v_ref, nxq_ref, wq_ref, wo_ref, nmlp_ref, wup_ref, wdown_ref,
                     nfin_ref, o_ref, h2_ref, hn2_ref, att_ref, *, final_norm):
    s = pl.program_id(0)

    @pl.when(s == 0)
    def _():
        h2_ref[...] = jnp.zeros_like(h2_ref)
        hn2_ref[...] = jnp.zeros_like(hn2_ref)

    n_blk = D_FF // FF_CHUNK
    assert n_blk >= XATTN_HEADS + 4
    T = h_ref.shape[0]
    h = h_ref[...]
    att = {}

    def head_cols(hd):
        return slice(hd * XATTN_HD, (hd + 1) * XATTN_HD)

    def attn_matmuls(i):
        if i == 1:
            att["q"] = _dot(att.pop("hn"), wq_ref[...])
        if 3 <= i < 3 + XATTN_HEADS:
            hd = i - 3
            att["o", hd] = _dot(att.pop(("p", hd)), v_ref[0, :, head_cols(hd)])
        if 2 <= i < 2 + XATTN_HEADS:
            hd = i - 2
            att["s", hd] = _dot_nt(att["qb"][:, head_cols(hd)], k_ref[0, :, head_cols(hd)])
        if i == 3 + XATTN_HEADS:
            for r in range(0, T, T // 2):
                att["proj", r] = _dot(att_ref[r:r + T // 2, :], wo_ref[...])

    def attn_vectors(i):
        if i == 0:
            att["hn"] = _rms(h, nxq_ref[...]).astype(BF)
        if i == 1:
            att["qb"] = att.pop("q").astype(BF)
        if 3 <= i < 3 + XATTN_HEADS:
            hd = i - 3
            att_ref[:, head_cols(hd)] = att.pop(("o", hd)).astype(BF)
        if 2 <= i < 2 + XATTN_HEADS:
            hd = i - 2
            sc = att.pop(("s", hd))
            p = jnp.exp(sc - jnp.max(sc, axis=-1, keepdims=True))
            att["p", hd] = (p * (1.0 / jnp.sum(p, axis=-1, keepdims=True))).astype(BF)
        if i == 3 + XATTN_HEADS:
            for r in range(0, T, T // 2):
                rows = slice(r, r + T // 2)
                h2 = h_ref[rows, :] + att.pop(("proj", r))
                h2_ref[rows, :] = h2
                hn2_ref[rows, :] = _rms(h2, nmlp_ref[...]).astype(BF)

    acc = None
    up = act = None
    for i in range(n_blk):
        attn_matmuls(i)
        if i >= 1:
            down = _dot(act, wdown_ref[(i - 1) * FF_CHUNK:i * FF_CHUNK, :])
            acc = h2_ref[...] + down if i == 1 else acc + down
        up = _dot(hn2_ref[...], wup_ref[:, i * FF_CHUNK:(i + 1) * FF_CHUNK])
        attn_vectors(i)
        a = jnp.maximum(up, 0.0)
        act = (a * a).astype(BF)
    halves = [slice(r, r + T // 2) for r in range(0, T, T // 2)]
    downs = [_dot(act[rows, :], wdown_ref[(n_blk - 1) * FF_CHUNK:, :]) for rows in halves]
    for rows, down in zip(halves, downs):
        out = acc[rows, :] + down
        o_ref[rows, :] = _rms(out, nfin_ref[...]) if final_norm else out


def _attn_mlp_call(h, k, v, norm_xq, wq, wo, norm_mlp, w_up, w_down, norm_final, final_norm):
    B, S, D = h.shape
    M = k.shape[1]
    T = FFN_TILE
    tiles_per_seq = S // T
    n_tiles = B * tiles_per_seq

    def attn_tile(s):
        return jnp.minimum(s, n_tiles - 1)

    out = pl.pallas_call(
        functools.partial(_attn_mlp_kernel, final_norm=final_norm),
        grid=(n_tiles + 1,),
        in_specs=[
            pl.BlockSpec((T, D), lambda s: (attn_tile(s), 0)),
            pl.BlockSpec((1, M, D), lambda s: (attn_tile(s) // tiles_per_seq, 0, 0)),
            pl.BlockSpec((1, M, D), lambda s: (attn_tile(s) // tiles_per_seq, 0, 0)),
            _const_spec((1, D)),
            _const_spec((D, D)),
            _const_spec((D, D)),
            _const_spec((1, D)),
            _const_spec((D, D_FF)),
            _const_spec((D_FF, D)),
            _const_spec((1, D)),
        ],
        out_specs=pl.BlockSpec((T, D), lambda s: (jnp.maximum(s - 1, 0), 0)),
        out_shape=jax.ShapeDtypeStruct((B * S, D), F32),
        scratch_shapes=[
            pltpu.VMEM((T, D), F32),
            pltpu.VMEM((T, D), BF),
            pltpu.VMEM((T, D), BF),
        ],
        compiler_params=pltpu.CompilerParams(
            dimension_semantics=("arbitrary",), vmem_limit_bytes=VMEM_LIMIT),
        name="xattn_mlp",
    )(h.reshape(B * S, D), k, v, norm_xq, wq, wo, norm_mlp, w_up, w_down, norm_final)
    return out.reshape(B, S, D)


def kernel(x, mem, norm_mix, w_in, pool_w, pool_scale, lb_theta, hgrn_norm, w_out, norm_xq, norm_mem,
           xw_q, xw_kv, xw_o, norm_mlp, w_up, w_down, norm_final):
    depth = norm_mix.shape[0]
    h = x
    for l in range(depth):
        k, v = _kv_call(mem, norm_mem[l][None], xw_kv[l].astype(BF))
        h = _mixer_call(h, norm_mix[l][None], w_in[l].astype(BF), pool_w[l].astype(BF),
                        pool_scale[l][None], lb_theta, hgrn_norm[l][None], w_out[l].astype(BF), l)
        h = _attn_mlp_call(h, k, v, norm_xq[l][None], xw_q[l].astype(BF), xw_o[l].astype(BF),
                           norm_mlp[l][None], w_up[l].astype(BF), w_down[l].astype(BF),
                           norm_final[None], final_norm=(l == depth - 1))
    return h
```

```python
import functools
import math

import jax
import jax.numpy as jnp
from jax import lax
from jax.experimental import pallas as pl
from jax.experimental.pallas import tpu as pltpu

D_MODEL = 1024
CHUNK = 64
POOL_WIDTH = 512
POOL_WINDOWS = (2, 4, 8, 16)
POOL_GW = POOL_WIDTH // len(POOL_WINDOWS)
POOL_HIST = 16
HGRN_HEADS = 4
HGRN_DK = 128
HGRN_W = HGRN_HEADS * HGRN_DK
IN_COLS = POOL_WIDTH + 4 * HGRN_W
XATTN_HEADS = 4
XATTN_HD = D_MODEL // XATTN_HEADS
D_FF = 4 * D_MODEL
EPS = 1e-6

MXU_COLS = 256
MIX_TILE = 512
FFN_TILE = 512
FF_CHUNK = 512
VMEM_LIMIT = 56 * 1024 * 1024

BF = jnp.bfloat16
F32 = jnp.float32


def _dot(a, b):
    return jnp.dot(a, b, preferred_element_type=F32)


def _dot_nt(a, b):
    return lax.dot_general(a, b, (((1,), (1,)), ((), ())), preferred_element_type=F32)


def _dot_tn(a, b):
    return lax.dot_general(a, b, (((0,), (0,)), ((), ())), preferred_element_type=F32)


def _rms(x, gain_row):
    ms = jnp.mean(x * x, axis=-1, keepdims=True)
    return x * lax.rsqrt(ms + EPS) * gain_row


def _const_spec(shape):
    return pl.BlockSpec(shape, lambda *_: (0,) * len(shape), pipeline_mode=pl.Buffered(1))


def _kv_kernel(mem_ref, nmem_ref, wkv_ref, k_ref, v_ref):
    mn = _rms(mem_ref[0], nmem_ref[...]).astype(BF)
    k_ref[0] = (_dot(mn, wkv_ref[:, :D_MODEL]) * (1.0 / math.sqrt(XATTN_HD))).astype(BF)
    v_ref[0] = _dot(mn, wkv_ref[:, D_MODEL:]).astype(BF)


def _kv_call(mem, norm_mem, wkv):
    B, M, D = mem.shape
    return pl.pallas_call(
        _kv_kernel,
        grid=(B,),
        in_specs=[
            pl.BlockSpec((1, M, D), lambda b: (b, 0, 0)),
            _const_spec((1, D)),
            _const_spec((D, 2 * D)),
        ],
        out_specs=[
            pl.BlockSpec((1, M, D), lambda b: (b, 0, 0)),
            pl.BlockSpec((1, M, D), lambda b: (b, 0, 0)),
        ],
        out_shape=[jax.ShapeDtypeStruct((B, M, D), BF)] * 2,
        compiler_params=pltpu.CompilerParams(
            dimension_semantics=("arbitrary",), vmem_limit_bytes=VMEM_LIMIT),
        name="xattn_kv",
    )(mem, norm_mem, wkv)


def _head_blocks(a, width):
    zero = jnp.zeros((CHUNK, width), a.dtype)
    return jnp.concatenate([
        jnp.concatenate([a[:, h * width:(h + 1) * width] if h == r else zero
                         for h in range(HGRN_HEADS)], axis=1)
        for r in range(HGRN_HEADS)], axis=0)


def _pool_group(u_ref, y_ref, hist, pos0, poolw_ref, pscale_ref, g):
    T = u_ref.shape[0]
    w = POOL_WINDOWS[g]
    t_pos = pos0 + lax.broadcasted_iota(jnp.int32, (T, POOL_GW), 0)
    cols = slice(g * POOL_GW, (g + 1) * POOL_GW)
    e = jnp.concatenate([hist[:, cols], u_ref[:, cols]], axis=0)
    s = e
    sh = 1
    while sh < w:
        s = s + pltpu.roll(s, sh, axis=0)
        sh *= 2
    cnt = jnp.minimum(t_pos + 1, w).astype(F32)
    yp = s[POOL_HIST:, :] / cnt - e[POOL_HIST:, :]
    yg = _dot(yp.astype(BF), poolw_ref[g]) * pscale_ref[:, cols]
    y_ref[:, cols] = yg.astype(BF)


def _hgrn_gates(u_ref, r0, f_mid, f_amp):
    rows = slice(r0, r0 + CHUNK)
    zq = u_ref[rows, POOL_WIDTH:POOL_WIDTH + HGRN_W]
    zf = u_ref[rows, POOL_WIDTH + HGRN_W:POOL_WIDTH + 2 * HGRN_W]
    zi = u_ref[rows, POOL_WIDTH + 2 * HGRN_W:POOL_WIDTH + 3 * HGRN_W]
    zg = u_ref[rows, POOL_WIDTH + 3 * HGRN_W:POOL_WIDTH + 4 * HGRN_W]
    f = f_mid + f_amp * jnp.tanh(0.5 * zf)
    l2f = jnp.log2(f)
    hq = 0.5 * zq
    hg = 0.5 * zg
    hi = l2f.astype(BF)
    lo = (l2f - hi.astype(F32)).astype(BF)
    return {"kk": 1.0 - f, "q": hq + hq * jnp.tanh(hq), "gate": hg + hg * jnp.tanh(hg),
            "vb": zi.astype(BF), "l2f_split": jnp.concatenate([hi, lo], axis=0)}


def _hgrn_cumsum(ch, tri2):
    ch["G"] = _dot(tri2, ch.pop("l2f_split"))


def _hgrn_decays(ch):
    G = ch.pop("G")
    g_mid = G[CHUNK // 2 - 1:CHUNK // 2, :]
    g_last = G[CHUNK - 1:CHUNK, :]
    e_fwd = jnp.exp2(G - g_mid)
    q_rel = ch.pop("q") * e_fwd
    k_rel = ch.pop("kk") * (1.0 / e_fwd)
    k_end = (k_rel * jnp.exp2(g_last - g_mid)).astype(BF)
    vb = ch.pop("vb")
    ch["q_dec"] = (q_rel * jnp.exp2(g_mid)).astype(BF)
    ch["q_rel"] = q_rel.astype(BF)
    ch["k_rel_blocks"] = _head_blocks(k_rel, HGRN_DK).T.astype(BF)
    ch["k_end"] = k_end
    ch["vb"] = vb
    ch["decay"] = jnp.exp2(g_last)


def _pair_blocks(a, hp, width):
    zero = jnp.zeros((a.shape[0], width), a.dtype)
    a0 = a[:, hp * width:(hp + 1) * width]
    a1 = a[:, (hp + 1) * width:(hp + 2) * width]
    return jnp.concatenate([jnp.concatenate([a0, zero], axis=1),
                            jnp.concatenate([zero, a1], axis=1)], axis=0)


def _hgrn_scores(ch):
    ch["sc"] = _dot(ch.pop("q_rel"), ch.pop("k_rel_blocks"))


def _state_operands(st):
    return {hp: _pair_blocks(st, hp, HGRN_DK).T.astype(BF) for hp in range(0, HGRN_HEADS, 2)}


def _hgrn_inter(ch, st_ops, hp):
    pair = slice(hp * HGRN_DK, (hp + 2) * HGRN_DK)
    ch["o_inter", hp] = _dot(ch["q_dec"][:, pair], st_ops[hp])


def _hgrn_state_update(ch, hp):
    vb = ch["vb"]
    v_pair = jnp.concatenate([vb[:, hp * HGRN_DK:(hp + 1) * HGRN_DK],
                              vb[:, (hp + 1) * HGRN_DK:(hp + 2) * HGRN_DK]], axis=0)
    ch["d_st", hp] = _dot_tn(v_pair, _pair_blocks(ch["k_end"], hp, HGRN_DK))


def _hgrn_mask_and_state(ch, st, causal):
    ch["sc"] = jnp.where(causal, ch["sc"], 0.0).astype(BF)
    del ch["q_dec"], ch["k_end"]
    d_st = jnp.concatenate([ch.pop(("d_st", hp)) for hp in range(0, HGRN_HEADS, 2)], axis=1)
    return st * ch.pop("decay") + d_st


def _hgrn_intra(ch, hp):
    ch["o", hp] = _dot(ch["sc"][:, hp * CHUNK:(hp + 2) * CHUNK], _pair_blocks(ch["vb"], hp, HGRN_DK))


def _hgrn_out(ch, y_ref, r0, hnorm_ref):
    del ch["sc"], ch["vb"]
    o = jnp.concatenate([ch.pop(("o", hp)) + ch.pop(("o_inter", hp)) for hp in range(0, HGRN_HEADS, 2)], axis=1)
    gate = ch.pop("gate")
    for h in range(HGRN_HEADS):
        hs = slice(h * HGRN_DK, (h + 1) * HGRN_DK)
        oh = o[:, hs]
        on = oh * lax.rsqrt(jnp.mean(oh * oh, axis=-1, keepdims=True) + EPS) * hnorm_ref[:, hs]
        y_ref[r0:r0 + CHUNK, POOL_WIDTH + h * HGRN_DK:POOL_WIDTH + (h + 1) * HGRN_DK] = (
            on * gate[:, hs]).astype(BF)


def _mixer_kernel(xp_ref, xn_ref, nmix_ref, win_ref, poolw_ref, pscale_ref, lbt_ref, hnorm_ref, wout_ref,
                  o_ref, ua_ref, ub_ref, ya_ref, yb_ref, st_ref, hist_ref, *, layer, tiles_per_seq):
    T = xn_ref.shape[0]
    cpt = T // CHUNK
    n_chunks = 2 * cpt
    g = pl.program_id(0)

    def project_block(xn, u_ref, blk):
        cols = slice(blk * MXU_COLS, (blk + 1) * MXU_COLS)
        u_ref[:, cols] = _dot(xn, win_ref[:, cols])

    def out_block(rows, y_ref, blk):
        cols = slice(blk * MXU_COLS, (blk + 1) * MXU_COLS)
        o_ref[rows, cols] = xp_ref[rows, cols] + _dot(y_ref[...], wout_ref[:, cols])

    n_in_blocks = IN_COLS // MXU_COLS
    n_out_blocks = D_MODEL // MXU_COLS

    @pl.when(g == 0)
    def _():
        xn0 = _rms(xp_ref[0:T, :], nmix_ref[...]).astype(BF)
        for blk in range(n_in_blocks):
            project_block(xn0, ua_ref, blk)
        st_ref[...] = jnp.zeros_like(st_ref)
        hist_ref[...] = jnp.zeros_like(hist_ref)

    th = lbt_ref[...]
    ex = jnp.exp(th - jnp.max(th, axis=0, keepdims=True))
    p = ex / jnp.sum(ex, axis=0, keepdims=True)
    lb = jnp.sum(p[0:layer + 1, :], axis=0, keepdims=True)
    f_mid = 0.5 * (1.0 + lb)
    f_amp = 0.5 * (1.0 - lb)
    row = lax.broadcasted_iota(jnp.int32, (CHUNK, HGRN_HEADS * CHUNK), 0)
    col = lax.broadcasted_iota(jnp.int32, (CHUNK, HGRN_HEADS * CHUNK), 1)
    causal = row >= (col & (CHUNK - 1))
    tri2 = jnp.where(causal[:, :2 * CHUNK], 1.0, 0.0).astype(BF)

    j0 = lax.rem(2 * g, tiles_per_seq)
    st = jnp.where(j0 == 0, 0.0, st_ref[...])
    hist = jnp.where(j0 == 0, 0.0, hist_ref[...])

    def chunk_src(c):
        return (ua_ref, ya_ref, c * CHUNK) if c < cpt else (ub_ref, yb_ref, (c - cpt) * CHUNK)

    early = list(range(-1, cpt - 1))
    late = list(range(cpt - 1, n_chunks + 2))
    proj_b_slots = {s: [] for s in early}
    for blk in range(n_in_blocks):
        proj_b_slots[early[max(0, blk - 1) * len(early) // (n_in_blocks - 1)]].append(blk)
    proj_a_slots = {s: [] for s in late}
    out_a_slots = {s: [] for s in late}
    proj_a_todo = list(range(n_in_blocks))
    out_a_todo = list(range(n_out_blocks))
    n_late_jobs = n_in_blocks + n_out_blocks
    for i, s in enumerate(late):
        for k in range((i + 1) * n_late_jobs // len(late) - i * n_late_jobs // len(late)):
            if out_a_todo and s >= cpt + 2 and (k == 0 or not proj_a_todo):
                out_a_slots[s].append(out_a_todo.pop(0))
            else:
                proj_a_slots[s].append(proj_a_todo.pop(0))
    assert not out_a_todo and not proj_a_todo

    xn_b = _rms(xp_ref[T:2 * T, :], nmix_ref[...]).astype(BF)
    xn_a = None
    hist_b = None
    pool_b_slot0 = cpt // 2
    assert len(POOL_WINDOWS) <= cpt and pool_b_slot0 + len(POOL_WINDOWS) <= n_chunks + 2
    st_ops = _state_operands(st)
    chunks = [None] * n_chunks
    for slot in range(-1, n_chunks + 2):
        big = [functools.partial(project_block, xn_b, ub_ref, blk) for blk in proj_b_slots.get(slot, ())]
        big += [functools.partial(project_block, xn_a, ua_ref, blk) for blk in proj_a_slots.get(slot, ())]
        big += [functools.partial(out_block, slice(0, T), ya_ref, blk) for blk in out_a_slots.get(slot, ())]
        head_pairs = list(range(0, HGRN_HEADS, 2))
        small = [[] for _ in range(1 + len(head_pairs))]
        if 0 <= slot - 1 < n_chunks:
            small[0].append(functools.partial(_hgrn_scores, chunks[slot - 1]))
            for i, hp in enumerate(head_pairs):
                small[1 + i] += [functools.partial(_hgrn_inter, chunks[slot - 1], st_ops, hp),
                                 functools.partial(_hgrn_state_update, chunks[slot - 1], hp)]
        if 0 <= slot < n_chunks:
            small[0].append(functools.partial(_hgrn_cumsum, chunks[slot], tri2))
        if 0 <= slot - 2 < n_chunks:
            for i, hp in enumerate(head_pairs):
                small[1 + i].append(functools.partial(_hgrn_intra, chunks[slot - 2], hp))
        for job in small.pop(0):
            job()
        while big or small:
            if big:
                big.pop(0)()
            if small:
                for job in small.pop(0):
                    job()
        if slot == -1:
            hist_b = ua_ref[T - POOL_HIST:T, 0:POOL_WIDTH]
        if 0 <= slot + 1 < len(POOL_WINDOWS):
            _pool_group(ua_ref, ya_ref, hist, j0 * T, poolw_ref, pscale_ref, slot + 1)
        if 0 <= slot - pool_b_slot0 < len(POOL_WINDOWS):
            _pool_group(ub_ref, yb_ref, hist_b, (j0 + 1) * T, poolw_ref, pscale_ref, slot - pool_b_slot0)
        if slot == cpt - 2:
            xn_a = _rms(xn_ref[...], nmix_ref[...]).astype(BF)
        if 0 <= slot + 1 < n_chunks:
            u_ref, _, r0 = chunk_src(slot + 1)
            chunks[slot + 1] = _hgrn_gates(u_ref, r0, f_mid, f_amp)
        if 0 <= slot < n_chunks:
            _hgrn_decays(chunks[slot])
        if 0 <= slot - 1 < n_chunks:
            st = _hgrn_mask_and_state(chunks[slot - 1], st, causal)
            if slot < n_chunks:
                st_ops = _state_operands(st)
        if 0 <= slot - 2 < n_chunks:
            _, y_ref, r0 = chunk_src(slot - 2)
            _hgrn_out(chunks[slot - 2], y_ref, r0, hnorm_ref)

    for blk in range(n_out_blocks):
        out_block(slice(T, 2 * T), yb_ref, blk)
    st_ref[...] = st
    hist_ref[...] = ub_ref[T - POOL_HIST:T, 0:POOL_WIDTH]


def _mixer_call(x, norm_mix, w_in, pool_w, pool_scale, lb_theta, hgrn_norm, w_out, layer):
    B, S, D = x.shape
    T = MIX_TILE
    tiles_per_seq = S // T
    assert S % (2 * T) == 0
    n_tiles = B * tiles_per_seq
    x2 = x.reshape(B * S, D)
    out = pl.pallas_call(
        functools.partial(_mixer_kernel, layer=layer, tiles_per_seq=tiles_per_seq),
        grid=(n_tiles // 2,),
        in_specs=[
            pl.BlockSpec((2 * T, D), lambda g: (g, 0)),
            pl.BlockSpec((T, D), lambda g: (jnp.minimum(2 * g + 2, n_tiles - 1), 0)),
            _const_spec((1, D)),
            _const_spec((D, IN_COLS)),
            _const_spec(pool_w.shape),
            _const_spec((1, POOL_WIDTH)),
            _const_spec(lb_theta.shape),
            _const_spec((1, HGRN_W)),
            _const_spec((D, D)),
        ],
        out_specs=pl.BlockSpec((2 * T, D), lambda g: (g, 0)),
        out_shape=jax.ShapeDtypeStruct((B * S, D), F32),
        scratch_shapes=[
            pltpu.VMEM((T, IN_COLS), F32),
            pltpu.VMEM((T, IN_COLS), F32),
            pltpu.VMEM((T, D), BF),
            pltpu.VMEM((T, D), BF),
            pltpu.VMEM((HGRN_DK, HGRN_W), F32),
            pltpu.VMEM((POOL_HIST, POOL_WIDTH), F32),
        ],
        compiler_params=pltpu.CompilerParams(
            dimension_semantics=("arbitrary",), vmem_limit_bytes=VMEM_LIMIT),
        name="token_mixer",
    )(x2, x2, norm_mix, w_in, pool_w, pool_scale, lb_theta, hgrn_norm, w_out)
    return out.reshape(B, S, D)


def _attn_mlp_kernel(h_ref, k_ref, v_ref, nxq_ref, wq_ref, wo_ref, nmlp_ref, wup_ref, wdown_ref,
                     nfin_ref, o_ref, h2_ref, hn2_ref, att_ref, *, final_norm):
    s = pl.program_id(0)

    @pl.when(s == 0)
    def _():
        h2_ref[...] = jnp.zeros_like(h2_ref)
        hn2_ref[...] = jnp.zeros_like(hn2_ref)

    n_blk = D_FF // FF_CHUNK
    assert n_blk >= XATTN_HEADS + 4
    T = h_ref.shape[0]
    h = h_ref[...]
    att = {}

    def head_cols(hd):
        return slice(hd * XATTN_HD, (hd + 1) * XATTN_HD)

    def attn_matmuls(i):
        if i == 1:
            att["q"] = _dot(att.pop("hn"), wq_ref[...])
        if 3 <= i < 3 + XATTN_HEADS:
            hd = i - 3
            att["o", hd] = _dot(att.pop(("p", hd)), v_ref[0, :, head_cols(hd)])
        if 2 <= i < 2 + XATTN_HEADS:
            hd = i - 2
            att["s", hd] = _dot_nt(att["qb"][:, head_cols(hd)], k_ref[0, :, head_cols(hd)])
        if i == 3 + XATTN_HEADS:
            for r in range(0, T, T // 2):
                att["proj", r] = _dot(att_ref[r:r + T // 2, :], wo_ref[...])

    def attn_vectors(i):
        if i == 0:
            att["hn"] = _rms(h, nxq_ref[...]).astype(BF)
        if i == 1:
            att["qb"] = att.pop("q").astype(BF)
        if 3 <= i < 3 + XATTN_HEADS:
            hd = i - 3
            att_ref[:, head_cols(hd)] = att.pop(("o", hd)).astype(BF)
        if 2 <= i < 2 + XATTN_HEADS:
            hd = i - 2
            sc = att.pop(("s", hd))
            p = jnp.exp(sc - jnp.max(sc, axis=-1, keepdims=True))
            att["p", hd] = (p * (1.0 / jnp.sum(p, axis=-1, keepdims=True))).astype(BF)
        if i == 3 + XATTN_HEADS:
            for r in range(0, T, T // 2):
                rows = slice(r, r + T // 2)
                h2 = h_ref[rows, :] + att.pop(("proj", r))
                h2_ref[rows, :] = h2
                hn2_ref[rows, :] = _rms(h2, nmlp_ref[...]).astype(BF)

    acc = None
    up = act = None
    for i in range(n_blk):
        attn_matmuls(i)
        if i >= 1:
            down = _dot(act, wdown_ref[(i - 1) * FF_CHUNK:i * FF_CHUNK, :])
            acc = h2_ref[...] + down if i == 1 else acc + down
        up = _dot(hn2_ref[...], wup_ref[:, i * FF_CHUNK:(i + 1) * FF_CHUNK])
        attn_vectors(i)
        a = jnp.maximum(up, 0.0)
        act = (a * a).astype(BF)
    halves = [slice(r, r + T // 2) for r in range(0, T, T // 2)]
    downs = [_dot(act[rows, :], wdown_ref[(n_blk - 1) * FF_CHUNK:, :]) for rows in halves]
    for rows, down in zip(halves, downs):
        out = acc[rows, :] + down
        o_ref[rows, :] = _rms(out, nfin_ref[...]) if final_norm else out


def _attn_mlp_call(h, k, v, norm_xq, wq, wo, norm_mlp, w_up, w_down, norm_final, final_norm):
    B, S, D = h.shape
    M = k.shape[1]
    T = FFN_TILE
    tiles_per_seq = S // T
    n_tiles = B * tiles_per_seq

    def attn_tile(s):
        return jnp.minimum(s, n_tiles - 1)

    out = pl.pallas_call(
        functools.partial(_attn_mlp_kernel, final_norm=final_norm),
        grid=(n_tiles + 1,),
        in_specs=[
            pl.BlockSpec((T, D), lambda s: (attn_tile(s), 0)),
            pl.BlockSpec((1, M, D), lambda s: (attn_tile(s) // tiles_per_seq, 0, 0)),
            pl.BlockSpec((1, M, D), lambda s: (attn_tile(s) // tiles_per_seq, 0, 0)),
            _const_spec((1, D)),
            _const_spec((D, D)),
            _const_spec((D, D)),
            _const_spec((1, D)),
            _const_spec((D, D_FF)),
            _const_spec((D_FF, D)),
            _const_spec((1, D)),
        ],
        out_specs=pl.BlockSpec((T, D), lambda s: (jnp.maximum(s - 1, 0), 0)),
        out_shape=jax.ShapeDtypeStruct((B * S, D), F32),
        scratch_shapes=[
            pltpu.VMEM((T, D), F32),
            pltpu.VMEM((T, D), BF),
            pltpu.VMEM((T, D), BF),
        ],
        compiler_params=pltpu.CompilerParams(
            dimension_semantics=("arbitrary",), vmem_limit_bytes=VMEM_LIMIT),
        name="xattn_mlp",
    )(h.reshape(B * S, D), k, v, norm_xq, wq, wo, norm_mlp, w_up, w_down, norm_final)
    return out.reshape(B, S, D)


def kernel(x, mem, norm_mix, w_in, pool_w, pool_scale, lb_theta, hgrn_norm, w_out, norm_xq, norm_mem,
           xw_q, xw_kv, xw_o, norm_mlp, w_up, w_down, norm_final):
    depth = norm_mix.shape[0]
    h = x
    for l in range(depth):
        k, v = _kv_call(mem, norm_mem[l][None], xw_kv[l].astype(BF))
        h = _mixer_call(h, norm_mix[l][None], w_in[l].astype(BF), pool_w[l].astype(BF),
                        pool_scale[l][None], lb_theta, hgrn_norm[l][None], w_out[l].astype(BF), l)
        h = _attn_mlp_call(h, k, v, norm_xq[l][None], xw_q[l].astype(BF), xw_o[l].astype(BF),
                           norm_mlp[l][None], w_up[l].astype(BF), w_down[l].astype(BF),
                           norm_final[None], final_norm=(l == depth - 1))
    return h
```

```python
import functools
import math

import jax
import jax.numpy as jnp
from jax import lax
from jax.experimental import pallas as pl
from jax.experimental.pallas import tpu as pltpu

D_MODEL = 1024
CHUNK = 64
POOL_WIDTH = 512
POOL_WINDOWS = (2, 4, 8, 16)
POOL_GW = POOL_WIDTH // len(POOL_WINDOWS)
POOL_HIST = 16
HGRN_HEADS = 4
HGRN_DK = 128
HGRN_W = HGRN_HEADS * HGRN_DK
IN_COLS = POOL_WIDTH + 4 * HGRN_W
XATTN_HEADS = 4
XATTN_HD = D_MODEL // XATTN_HEADS
D_FF = 4 * D_MODEL
EPS = 1e-6

MXU_COLS = 256
MIX_TILE = 512
FFN_TILE = 512
FF_CHUNK = 512
VMEM_LIMIT = 56 * 1024 * 1024

BF = jnp.bfloat16
F32 = jnp.float32


def _dot(a, b):
    return jnp.dot(a, b, preferred_element_type=F32)


def _dot_nt(a, b):
    return lax.dot_general(a, b, (((1,), (1,)), ((), ())), preferred_element_type=F32)


def _dot_tn(a, b):
    return lax.dot_general(a, b, (((0,), (0,)), ((), ())), preferred_element_type=F32)


def _rms(x, gain_row):
    ms = jnp.mean(x * x, axis=-1, keepdims=True)
    return x * lax.rsqrt(ms + EPS) * gain_row


def _const_spec(shape):
    return pl.BlockSpec(shape, lambda *_: (0,) * len(shape), pipeline_mode=pl.Buffered(1))


def _kv_kernel(mem_ref, nmem_ref, wkv_ref, wq_ref, wo_ref, qk_ref, vw_ref):
    M = mem_ref.shape[1]
    mn = _rms(mem_ref[0], nmem_ref[...]).astype(BF)
    kb = (_dot(mn, wkv_ref[:, :D_MODEL]) * (1.0 / math.sqrt(XATTN_HD))).astype(BF)
    vb = _dot(mn, wkv_ref[:, D_MODEL:]).astype(BF)
    for hd in range(XATTN_HEADS):
        cols = slice(hd * XATTN_HD, (hd + 1) * XATTN_HD)
        qk_ref[0, :, hd * M:(hd + 1) * M] = _dot_nt(wq_ref[:, cols], kb[:, cols]).astype(BF)
        vw_ref[0, hd * M:(hd + 1) * M, :] = _dot(vb[:, cols], wo_ref[cols, :]).astype(BF)


def _kv_call(mem, norm_mem, wkv, wq, wo):
    B, M, D = mem.shape
    return pl.pallas_call(
        _kv_kernel,
        grid=(B,),
        in_specs=[
            pl.BlockSpec((1, M, D), lambda b: (b, 0, 0)),
            _const_spec((1, D)),
            _const_spec((D, 2 * D)),
            _const_spec((D, D)),
            _const_spec((D, D)),
        ],
        out_specs=[
            pl.BlockSpec((1, D, XATTN_HEADS * M), lambda b: (b, 0, 0)),
            pl.BlockSpec((1, XATTN_HEADS * M, D), lambda b: (b, 0, 0)),
        ],
        out_shape=[jax.ShapeDtypeStruct((B, D, XATTN_HEADS * M), BF),
                   jax.ShapeDtypeStruct((B, XATTN_HEADS * M, D), BF)],
        compiler_params=pltpu.CompilerParams(
            dimension_semantics=("arbitrary",), vmem_limit_bytes=VMEM_LIMIT),
        name="xattn_kv",
    )(mem, norm_mem, wkv, wq, wo)


def _head_blocks(a, width):
    zero = jnp.zeros((CHUNK, width), a.dtype)
    return jnp.concatenate([
        jnp.concatenate([a[:, h * width:(h + 1) * width] if h == r else zero
                         for h in range(HGRN_HEADS)], axis=1)
        for r in range(HGRN_HEADS)], axis=0)


def _pool_group(u_ref, y_ref, hist, pos0, poolw_ref, pscale_ref, g):
    T = u_ref.shape[0]
    w = POOL_WINDOWS[g]
    t_pos = pos0 + lax.broadcasted_iota(jnp.int32, (T, POOL_GW), 0)
    cols = slice(g * POOL_GW, (g + 1) * POOL_GW)
    e = jnp.concatenate([hist[:, cols], u_ref[:, cols]], axis=0)
    s = e
    sh = 1
    while sh < w:
        s = s + pltpu.roll(s, sh, axis=0)
        sh *= 2
    cnt = jnp.minimum(t_pos + 1, w).astype(F32)
    yp = s[POOL_HIST:, :] / cnt - e[POOL_HIST:, :]
    yg = _dot(yp.astype(BF), poolw_ref[g]) * pscale_ref[:, cols]
    y_ref[:, cols] = yg.astype(BF)


def _hgrn_gates(u_ref, r0, f_mid, f_amp):
    rows = slice(r0, r0 + CHUNK)
    zq = u_ref[rows, POOL_WIDTH:POOL_WIDTH + HGRN_W]
    zf = u_ref[rows, POOL_WIDTH + HGRN_W:POOL_WIDTH + 2 * HGRN_W]
    zi = u_ref[rows, POOL_WIDTH + 2 * HGRN_W:POOL_WIDTH + 3 * HGRN_W]
    zg = u_ref[rows, POOL_WIDTH + 3 * HGRN_W:POOL_WIDTH + 4 * HGRN_W]
    f = f_mid + f_amp * jnp.tanh(0.5 * zf)
    l2f = jnp.log2(f)
    hq = 0.5 * zq
    hg = 0.5 * zg
    hi = l2f.astype(BF)
    lo = (l2f - hi.astype(F32)).astype(BF)
    return {"kk": 1.0 - f, "q": hq + hq * jnp.tanh(hq), "gate": hg + hg * jnp.tanh(hg),
            "vb": zi.astype(BF), "l2f_split": jnp.concatenate([hi, lo], axis=0)}


def _hgrn_cumsum(ch, tri2):
    ch["G"] = _dot(tri2, ch.pop("l2f_split"))


def _hgrn_decays(ch):
    G = ch.pop("G")
    g_mid = G[CHUNK // 2 - 1:CHUNK // 2, :]
    g_last = G[CHUNK - 1:CHUNK, :]
    e_fwd = jnp.exp2(G - g_mid)
    q_rel = ch.pop("q") * e_fwd
    k_rel = ch.pop("kk") * (1.0 / e_fwd)
    k_end = (k_rel * jnp.exp2(g_last - g_mid)).astype(BF)
    vb = ch.pop("vb")
    ch["q_dec"] = (q_rel * jnp.exp2(g_mid)).astype(BF)
    ch["q_rel"] = q_rel.astype(BF)
    ch["k_rel_blocks"] = _head_blocks(k_rel, HGRN_DK).T.astype(BF)
    ch["k_end"] = k_end
    ch["vb"] = vb
    ch["decay"] = jnp.exp2(g_last)


def _pair_blocks(a, hp, width):
    zero = jnp.zeros((a.shape[0], width), a.dtype)
    a0 = a[:, hp * width:(hp + 1) * width]
    a1 = a[:, (hp + 1) * width:(hp + 2) * width]
    return jnp.concatenate([jnp.concatenate([a0, zero], axis=1),
                            jnp.concatenate([zero, a1], axis=1)], axis=0)


def _hgrn_scores(ch):
    ch["sc"] = _dot(ch.pop("q_rel"), ch.pop("k_rel_blocks"))


def _state_operands(st):
    return {hp: _pair_blocks(st, hp, HGRN_DK).T.astype(BF) for hp in range(0, HGRN_HEADS, 2)}


def _hgrn_inter(ch, st_ops, hp):
    pair = slice(hp * HGRN_DK, (hp + 2) * HGRN_DK)
    ch["o_inter", hp] = _dot(ch["q_dec"][:, pair], st_ops[hp])


def _hgrn_state_update(ch, hp):
    vb = ch["vb"]
    v_pair = jnp.concatenate([vb[:, hp * HGRN_DK:(hp + 1) * HGRN_DK],
                              vb[:, (hp + 1) * HGRN_DK:(hp + 2) * HGRN_DK]], axis=0)
    ch["d_st", hp] = _dot_tn(v_pair, _pair_blocks(ch["k_end"], hp, HGRN_DK))


def _hgrn_mask_and_state(ch, st, causal):
    ch["sc"] = jnp.where(causal, ch["sc"], 0.0).astype(BF)
    del ch["q_dec"], ch["k_end"]
    d_st = jnp.concatenate([ch.pop(("d_st", hp)) for hp in range(0, HGRN_HEADS, 2)], axis=1)
    return st * ch.pop("decay") + d_st


def _hgrn_intra(ch, hp):
    ch["o", hp] = _dot(ch["sc"][:, hp * CHUNK:(hp + 2) * CHUNK], _pair_blocks(ch["vb"], hp, HGRN_DK))


def _hgrn_out(ch, y_ref, r0, hnorm_ref):
    del ch["sc"], ch["vb"]
    o = jnp.concatenate([ch.pop(("o", hp)) + ch.pop(("o_inter", hp)) for hp in range(0, HGRN_HEADS, 2)], axis=1)
    gate = ch.pop("gate")
    for h in range(HGRN_HEADS):
        hs = slice(h * HGRN_DK, (h + 1) * HGRN_DK)
        oh = o[:, hs]
        on = oh * lax.rsqrt(jnp.mean(oh * oh, axis=-1, keepdims=True) + EPS) * hnorm_ref[:, hs]
        y_ref[r0:r0 + CHUNK, POOL_WIDTH + h * HGRN_DK:POOL_WIDTH + (h + 1) * HGRN_DK] = (
            on * gate[:, hs]).astype(BF)


def _mixer_kernel(xp_ref, xn_ref, nmix_ref, win_ref, poolw_ref, pscale_ref, lbt_ref, hnorm_ref, wout_ref,
                  o_ref, ua_ref, ub_ref, ya_ref, yb_ref, st_ref, hist_ref, *, layer, tiles_per_seq):
    T = xn_ref.shape[0]
    cpt = T // CHUNK
    n_chunks = 2 * cpt
    g = pl.program_id(0)

    def project_block(xn, u_ref, blk):
        cols = slice(blk * MXU_COLS, (blk + 1) * MXU_COLS)
        u_ref[:, cols] = _dot(xn, win_ref[:, cols])

    def out_block(rows, y_ref, blk):
        cols = slice(blk * MXU_COLS, (blk + 1) * MXU_COLS)
        o_ref[rows, cols] = xp_ref[rows, cols] + _dot(y_ref[...], wout_ref[:, cols])

    n_in_blocks = IN_COLS // MXU_COLS
    n_out_blocks = D_MODEL // MXU_COLS

    @pl.when(g == 0)
    def _():
        xn0 = _rms(xp_ref[0:T, :], nmix_ref[...]).astype(BF)
        for blk in range(n_in_blocks):
            project_block(xn0, ua_ref, blk)
        st_ref[...] = jnp.zeros_like(st_ref)
        hist_ref[...] = jnp.zeros_like(hist_ref)

    th = lbt_ref[...]
    ex = jnp.exp(th - jnp.max(th, axis=0, keepdims=True))
    p = ex / jnp.sum(ex, axis=0, keepdims=True)
    lb = jnp.sum(p[0:layer + 1, :], axis=0, keepdims=True)
    f_mid = 0.5 * (1.0 + lb)
    f_amp = 0.5 * (1.0 - lb)
    row = lax.broadcasted_iota(jnp.int32, (CHUNK, HGRN_HEADS * CHUNK), 0)
    col = lax.broadcasted_iota(jnp.int32, (CHUNK, HGRN_HEADS * CHUNK), 1)
    causal = row >= (col & (CHUNK - 1))
    tri2 = jnp.where(causal[:, :2 * CHUNK], 1.0, 0.0).astype(BF)

    j0 = lax.rem(2 * g, tiles_per_seq)
    st = jnp.where(j0 == 0, 0.0, st_ref[...])
    hist = jnp.where(j0 == 0, 0.0, hist_ref[...])

    def chunk_src(c):
        return (ua_ref, ya_ref, c * CHUNK) if c < cpt else (ub_ref, yb_ref, (c - cpt) * CHUNK)

    early = list(range(-1, cpt - 1))
    late = list(range(cpt - 1, n_chunks + 2))
    proj_b_slots = {s: [] for s in early}
    for blk in range(n_in_blocks):
        proj_b_slots[early[max(0, blk - 1) * len(early) // (n_in_blocks - 1)]].append(blk)
    proj_a_slots = {s: [] for s in late}
    out_a_slots = {s: [] for s in late}
    proj_a_todo = list(range(n_in_blocks))
    out_a_todo = list(range(n_out_blocks))
    n_late_jobs = n_in_blocks + n_out_blocks
    for i, s in enumerate(late):
        for k in range((i + 1) * n_late_jobs // len(late) - i * n_late_jobs // len(late)):
            if out_a_todo and s >= cpt + 2 and (k == 0 or not proj_a_todo):
                out_a_slots[s].append(out_a_todo.pop(0))
            else:
                proj_a_slots[s].append(proj_a_todo.pop(0))
    assert not out_a_todo and not proj_a_todo

    xn_b = _rms(xp_ref[T:2 * T, :], nmix_ref[...]).astype(BF)
    xn_a = None
    hist_b = None
    pool_b_slot0 = cpt // 2
    assert len(POOL_WINDOWS) <= cpt and pool_b_slot0 + len(POOL_WINDOWS) <= n_chunks + 2
    st_ops = _state_operands(st)
    chunks = [None] * n_chunks
    for slot in range(-1, n_chunks + 2):
        big = [functools.partial(project_block, xn_b, ub_ref, blk) for blk in proj_b_slots.get(slot, ())]
        big += [functools.partial(project_block, xn_a, ua_ref, blk) for blk in proj_a_slots.get(slot, ())]
        big += [functools.partial(out_block, slice(0, T), ya_ref, blk) for blk in out_a_slots.get(slot, ())]
        head_pairs = list(range(0, HGRN_HEADS, 2))
        small = [[] for _ in range(1 + len(head_pairs))]
        if 0 <= slot - 1 < n_chunks:
            small[0].append(functools.partial(_hgrn_scores, chunks[slot - 1]))
            for i, hp in enumerate(head_pairs):
                small[1 + i] += [functools.partial(_hgrn_inter, chunks[slot - 1], st_ops, hp),
                                 functools.partial(_hgrn_state_update, chunks[slot - 1], hp)]
        if 0 <= slot < n_chunks:
            small[0].append(functools.partial(_hgrn_cumsum, chunks[slot], tri2))
        if 0 <= slot - 2 < n_chunks:
            for i, hp in enumerate(head_pairs):
                small[1 + i].append(functools.partial(_hgrn_intra, chunks[slot - 2], hp))
        for job in small.pop(0):
            job()
        while big or small:
            if big:
                big.pop(0)()
            if small:
                for job in small.pop(0):
                    job()
        if slot == -1:
            hist_b = ua_ref[T - POOL_HIST:T, 0:POOL_WIDTH]
        if 0 <= slot + 1 < len(POOL_WINDOWS):
            _pool_group(ua_ref, ya_ref, hist, j0 * T, poolw_ref, pscale_ref, slot + 1)
        if 0 <= slot - pool_b_slot0 < len(POOL_WINDOWS):
            _pool_group(ub_ref, yb_ref, hist_b, (j0 + 1) * T, poolw_ref, pscale_ref, slot - pool_b_slot0)
        if slot == cpt - 2:
            xn_a = _rms(xn_ref[...], nmix_ref[...]).astype(BF)
        if 0 <= slot + 1 < n_chunks:
            u_ref, _, r0 = chunk_src(slot + 1)
            chunks[slot + 1] = _hgrn_gates(u_ref, r0, f_mid, f_amp)
        if 0 <= slot < n_chunks:
            _hgrn_decays(chunks[slot])
        if 0 <= slot - 1 < n_chunks:
            st = _hgrn_mask_and_state(chunks[slot - 1], st, causal)
            if slot < n_chunks:
                st_ops = _state_operands(st)
        if 0 <= slot - 2 < n_chunks:
            _, y_ref, r0 = chunk_src(slot - 2)
            _hgrn_out(chunks[slot - 2], y_ref, r0, hnorm_ref)

    for blk in range(n_out_blocks):
        out_block(slice(T, 2 * T), yb_ref, blk)
    st_ref[...] = st
    hist_ref[...] = ub_ref[T - POOL_HIST:T, 0:POOL_WIDTH]


def _mixer_call(x, norm_mix, w_in, pool_w, pool_scale, lb_theta, hgrn_norm, w_out, layer):
    B, S, D = x.shape
    T = MIX_TILE
    tiles_per_seq = S // T
    assert S % (2 * T) == 0
    n_tiles = B * tiles_per_seq
    x2 = x.reshape(B * S, D)
    out = pl.pallas_call(
        functools.partial(_mixer_kernel, layer=layer, tiles_per_seq=tiles_per_seq),
        grid=(n_tiles // 2,),
        in_specs=[
            pl.BlockSpec((2 * T, D), lambda g: (g, 0)),
            pl.BlockSpec((T, D), lambda g: (jnp.minimum(2 * g + 2, n_tiles - 1), 0)),
            _const_spec((1, D)),
            _const_spec((D, IN_COLS)),
            _const_spec(pool_w.shape),
            _const_spec((1, POOL_WIDTH)),
            _const_spec(lb_theta.shape),
            _const_spec((1, HGRN_W)),
            _const_spec((D, D)),
        ],
        out_specs=pl.BlockSpec((2 * T, D), lambda g: (g, 0)),
        out_shape=jax.ShapeDtypeStruct((B * S, D), F32),
        scratch_shapes=[
            pltpu.VMEM((T, IN_COLS), F32),
            pltpu.VMEM((T, IN_COLS), F32),
            pltpu.VMEM((T, D), BF),
            pltpu.VMEM((T, D), BF),
            pltpu.VMEM((HGRN_DK, HGRN_W), F32),
            pltpu.VMEM((POOL_HIST, POOL_WIDTH), F32),
        ],
        compiler_params=pltpu.CompilerParams(
            dimension_semantics=("arbitrary",), vmem_limit_bytes=VMEM_LIMIT),
        name="token_mixer",
    )(x2, x2, norm_mix, w_in, pool_w, pool_scale, lb_theta, hgrn_norm, w_out)
    return out.reshape(B, S, D)


def _attn_mlp_kernel(h_ref, qk_ref, vw_ref, nxq_ref, nmlp_ref, wup_ref, wdown_ref,
                     nfin_ref, o_ref, h2_ref, hn2_ref, att_ref, *, final_norm):
    s = pl.program_id(0)

    @pl.when(s == 0)
    def _():
        h2_ref[...] = jnp.zeros_like(h2_ref)
        hn2_ref[...] = jnp.zeros_like(hn2_ref)

    n_blk = D_FF // FF_CHUNK
    assert n_blk >= XATTN_HEADS + 2
    T = h_ref.shape[0]
    M = qk_ref.shape[2] // XATTN_HEADS
    h = h_ref[...]
    att = {}

    def head_cols(hd):
        return slice(hd * M, (hd + 1) * M)

    def attn_matmuls(i):
        if i == 1:
            att["s"] = _dot(att.pop("hn"), qk_ref[0])
        if i == n_blk - 1:
            for r in range(0, T, T // 2):
                att["proj", r] = _dot(att_ref[r:r + T // 2, :], vw_ref[0])

    def attn_vectors(i):
        if i == 0:
            att["hn"] = _rms(h, nxq_ref[...]).astype(BF)
        if 1 <= i <= XATTN_HEADS:
            sc = att["s"][:, head_cols(i - 1)]
            p = jnp.exp(sc - jnp.max(sc, axis=-1, keepdims=True))
            att_ref[:, head_cols(i - 1)] = (p * (1.0 / jnp.sum(p, axis=-1, keepdims=True))).astype(BF)
        if i == n_blk - 1:
            for r in range(0, T, T // 2):
                rows = slice(r, r + T // 2)
                h2 = h_ref[rows, :] + att.pop(("proj", r))
                h2_ref[rows, :] = h2
                hn2_ref[rows, :] = _rms(h2, nmlp_ref[...]).astype(BF)

    acc = None
    up = act = None
    for i in range(n_blk):
        if i >= 1:
            down = _dot(act, wdown_ref[(i - 1) * FF_CHUNK:i * FF_CHUNK, :])
            acc = h2_ref[...] + down if i == 1 else acc + down
        attn_matmuls(i)
        up = _dot(hn2_ref[...], wup_ref[:, i * FF_CHUNK:(i + 1) * FF_CHUNK])
        a = jnp.maximum(up, 0.0)
        act = (a * a).astype(BF)
        attn_vectors(i)
    halves = [slice(r, r + T // 2) for r in range(0, T, T // 2)]
    downs = [_dot(act[rows, :], wdown_ref[(n_blk - 1) * FF_CHUNK:, :]) for rows in halves]
    for rows, down in zip(halves, downs):
        out = acc[rows, :] + down
        o_ref[rows, :] = _rms(out, nfin_ref[...]) if final_norm else out


def _attn_mlp_call(h, qk, vw, norm_xq, norm_mlp, w_up, w_down, norm_final, final_norm):
    B, S, D = h.shape
    HM = vw.shape[1]
    T = FFN_TILE
    tiles_per_seq = S // T
    n_tiles = B * tiles_per_seq

    def attn_tile(s):
        return jnp.minimum(s, n_tiles - 1)

    out = pl.pallas_call(
        functools.partial(_attn_mlp_kernel, final_norm=final_norm),
        grid=(n_tiles + 1,),
        in_specs=[
            pl.BlockSpec((T, D), lambda s: (attn_tile(s), 0)),
            pl.BlockSpec((1, D, HM), lambda s: (attn_tile(s) // tiles_per_seq, 0, 0)),
            pl.BlockSpec((1, HM, D), lambda s: (attn_tile(s) // tiles_per_seq, 0, 0)),
            _const_spec((1, D)),
            _const_spec((1, D)),
            _const_spec((D, D_FF)),
            _const_spec((D_FF, D)),
            _const_spec((1, D)),
        ],
        out_specs=pl.BlockSpec((T, D), lambda s: (jnp.maximum(s - 1, 0), 0)),
        out_shape=jax.ShapeDtypeStruct((B * S, D), F32),
        scratch_shapes=[
            pltpu.VMEM((T, D), F32),
            pltpu.VMEM((T, D), BF),
            pltpu.VMEM((T, HM), BF),
        ],
        compiler_params=pltpu.CompilerParams(
            dimension_semantics=("arbitrary",), vmem_limit_bytes=VMEM_LIMIT),
        name="xattn_mlp",
    )(h.reshape(B * S, D), qk, vw, norm_xq, norm_mlp, w_up, w_down, norm_final)
    return out.reshape(B, S, D)


def kernel(x, mem, norm_mix, w_in, pool_w, pool_scale, lb_theta, hgrn_norm, w_out, norm_xq, norm_mem,
           xw_q, xw_kv, xw_o, norm_mlp, w_up, w_down, norm_final):
    depth = norm_mix.shape[0]
    h = x
    for l in range(depth):
        qk, vw = _kv_call(mem, norm_mem[l][None], xw_kv[l].astype(BF), xw_q[l].astype(BF), xw_o[l].astype(BF))
        h = _mixer_call(h, norm_mix[l][None], w_in[l].astype(BF), pool_w[l].astype(BF),
                        pool_scale[l][None], lb_theta, hgrn_norm[l][None], w_out[l].astype(BF), l)
        h = _attn_mlp_call(h, qk, vw, norm_xq[l][None],
                           norm_mlp[l][None], w_up[l].astype(BF), w_down[l].astype(BF),
                           norm_final[None], final_norm=(l == depth - 1))
    return h
```

```python
import functools
import math

import jax
import jax.numpy as jnp
from jax import lax
from jax.experimental import pallas as pl
from jax.experimental.pallas import tpu as pltpu

D_MODEL = 1024
CHUNK = 64
POOL_WIDTH = 512
POOL_WINDOWS = (2, 4, 8, 16)
POOL_GW = POOL_WIDTH // len(POOL_WINDOWS)
POOL_HIST = 16
HGRN_HEADS = 4
HGRN_DK = 128
HGRN_W = HGRN_HEADS * HGRN_DK
IN_COLS = POOL_WIDTH + 4 * HGRN_W
XATTN_HEADS = 4
XATTN_HD = D_MODEL // XATTN_HEADS
D_FF = 4 * D_MODEL
EPS = 1e-6

MXU_COLS = 256
MIX_TILE = 512
FFN_TILE = 512
FF_CHUNK = 512
KV_ROWS = 2
VMEM_LIMIT = 56 * 1024 * 1024

BF = jnp.bfloat16
F32 = jnp.float32


def _dot(a, b):
    return jnp.dot(a, b, preferred_element_type=F32)


def _dot_nt(a, b):
    return lax.dot_general(a, b, (((1,), (1,)), ((), ())), preferred_element_type=F32)


def _dot_tn(a, b):
    return lax.dot_general(a, b, (((0,), (0,)), ((), ())), preferred_element_type=F32)


def _rms(x, gain_row):
    ms = jnp.mean(x * x, axis=-1, keepdims=True)
    return x * lax.rsqrt(ms + EPS) * gain_row


def _const_spec(shape):
    return pl.BlockSpec(shape, lambda *_: (0,) * len(shape), pipeline_mode=pl.Buffered(1))


def _kv_kernel(mem_ref, nmem_ref, wkv_ref, wq_ref, wo_ref, qk_ref, vw_ref):
    R, M, D = mem_ref.shape
    mn = _rms(mem_ref[...].reshape(R * M, D), nmem_ref[...]).astype(BF)
    kb = (_dot(mn, wkv_ref[:, :D_MODEL]) * (1.0 / math.sqrt(XATTN_HD))).astype(BF)
    vb = _dot(mn, wkv_ref[:, D_MODEL:]).astype(BF)
    for hd in range(XATTN_HEADS):
        cols = slice(hd * XATTN_HD, (hd + 1) * XATTN_HD)
        vw = _dot(vb[:, cols], wo_ref[cols, :]).astype(BF)
        for r in range(R):
            rows = slice(r * M, (r + 1) * M)
            qk_ref[r, :, hd * M:(hd + 1) * M] = _dot_nt(wq_ref[:, cols], kb[rows, cols]).astype(BF)
            vw_ref[r, hd * M:(hd + 1) * M, :] = vw[rows, :]


def _kv_call(mem, norm_mem, wkv, wq, wo):
    B, M, D = mem.shape
    R = KV_ROWS
    assert B % R == 0
    return pl.pallas_call(
        _kv_kernel,
        grid=(B // R,),
        in_specs=[
            pl.BlockSpec((R, M, D), lambda b: (b, 0, 0)),
            _const_spec((1, D)),
            _const_spec((D, 2 * D)),
            _const_spec((D, D)),
            _const_spec((D, D)),
        ],
        out_specs=[
            pl.BlockSpec((R, D, XATTN_HEADS * M), lambda b: (b, 0, 0)),
            pl.BlockSpec((R, XATTN_HEADS * M, D), lambda b: (b, 0, 0)),
        ],
        out_shape=[jax.ShapeDtypeStruct((B, D, XATTN_HEADS * M), BF),
                   jax.ShapeDtypeStruct((B, XATTN_HEADS * M, D), BF)],
        compiler_params=pltpu.CompilerParams(
            dimension_semantics=("arbitrary",), vmem_limit_bytes=VMEM_LIMIT),
        name="xattn_kv",
    )(mem, norm_mem, wkv, wq, wo)


def _head_blocks(a, width):
    zero = jnp.zeros((CHUNK, width), a.dtype)
    return jnp.concatenate([
        jnp.concatenate([a[:, h * width:(h + 1) * width] if h == r else zero
                         for h in range(HGRN_HEADS)], axis=1)
        for r in range(HGRN_HEADS)], axis=0)


def _pool_group(u_ref, y_ref, hist, pos0, poolw_ref, pscale_ref, g):
    T = u_ref.shape[0]
    w = POOL_WINDOWS[g]
    t_pos = pos0 + lax.broadcasted_iota(jnp.int32, (T, POOL_GW), 0)
    cols = slice(g * POOL_GW, (g + 1) * POOL_GW)
    e = jnp.concatenate([hist[:, cols], u_ref[:, cols]], axis=0)
    s = e
    sh = 1
    while sh < w:
        s = s + pltpu.roll(s, sh, axis=0)
        sh *= 2
    cnt = jnp.minimum(t_pos + 1, w).astype(F32)
    yp = s[POOL_HIST:, :] / cnt - e[POOL_HIST:, :]
    yg = _dot(yp.astype(BF), poolw_ref[g]) * pscale_ref[:, cols]
    y_ref[:, cols] = yg.astype(BF)


def _hgrn_gates(u_ref, r0, f_mid, f_amp):
    rows = slice(r0, r0 + CHUNK)
    zq = u_ref[rows, POOL_WIDTH:POOL_WIDTH + HGRN_W]
    zf = u_ref[rows, POOL_WIDTH + HGRN_W:POOL_WIDTH + 2 * HGRN_W]
    zi = u_ref[rows, POOL_WIDTH + 2 * HGRN_W:POOL_WIDTH + 3 * HGRN_W]
    zg = u_ref[rows, POOL_WIDTH + 3 * HGRN_W:POOL_WIDTH + 4 * HGRN_W]
    f = f_mid + f_amp * jnp.tanh(0.5 * zf)
    l2f = jnp.log2(f)
    hq = 0.5 * zq
    hg = 0.5 * zg
    hi = l2f.astype(BF)
    lo = (l2f - hi.astype(F32)).astype(BF)
    return {"kk": 1.0 - f, "q": hq + hq * jnp.tanh(hq), "gate": hg + hg * jnp.tanh(hg),
            "vb": zi.astype(BF), "l2f_split": jnp.concatenate([hi, lo], axis=0)}


def _hgrn_cumsum(ch, tri2):
    ch["G"] = _dot(tri2, ch.pop("l2f_split"))


def _hgrn_decays(ch):
    G = ch.pop("G")
    g_mid = G[CHUNK // 2 - 1:CHUNK // 2, :]
    g_last = G[CHUNK - 1:CHUNK, :]
    e_fwd = jnp.exp2(G - g_mid)
    q_rel = ch.pop("q") * e_fwd
    k_rel = ch.pop("kk") * (1.0 / e_fwd)
    k_end = (k_rel * jnp.exp2(g_last - g_mid)).astype(BF)
    vb = ch.pop("vb")
    ch["q_dec"] = (q_rel * jnp.exp2(g_mid)).astype(BF)
    ch["q_rel"] = q_rel.astype(BF)
    ch["k_rel_blocks"] = _head_blocks(k_rel, HGRN_DK).T.astype(BF)
    ch["k_end"] = k_end
    ch["vb"] = vb
    ch["decay"] = jnp.exp2(g_last)


def _pair_blocks(a, hp, width):
    zero = jnp.zeros((a.shape[0], width), a.dtype)
    a0 = a[:, hp * width:(hp + 1) * width]
    a1 = a[:, (hp + 1) * width:(hp + 2) * width]
    return jnp.concatenate([jnp.concatenate([a0, zero], axis=1),
                            jnp.concatenate([zero, a1], axis=1)], axis=0)


def _hgrn_scores(ch):
    ch["sc"] = _dot(ch.pop("q_rel"), ch.pop("k_rel_blocks"))


def _state_operands(st):
    return {hp: _pair_blocks(st, hp, HGRN_DK).T.astype(BF) for hp in range(0, HGRN_HEADS, 2)}


def _hgrn_inter(ch, st_ops, hp):
    pair = slice(hp * HGRN_DK, (hp + 2) * HGRN_DK)
    ch["o_inter", hp] = _dot(ch["q_dec"][:, pair], st_ops[hp])


def _hgrn_state_update(ch, hp):
    vb = ch["vb"]
    v_pair = jnp.concatenate([vb[:, hp * HGRN_DK:(hp + 1) * HGRN_DK],
                              vb[:, (hp + 1) * HGRN_DK:(hp + 2) * HGRN_DK]], axis=0)
    ch["d_st", hp] = _dot_tn(v_pair, _pair_blocks(ch["k_end"], hp, HGRN_DK))


def _hgrn_mask_and_state(ch, st, causal):
    ch["sc"] = jnp.where(causal, ch["sc"], 0.0).astype(BF)
    del ch["q_dec"], ch["k_end"]
    d_st = jnp.concatenate([ch.pop(("d_st", hp)) for hp in range(0, HGRN_HEADS, 2)], axis=1)
    return st * ch.pop("decay") + d_st


def _hgrn_intra(ch, hp):
    ch["o", hp] = _dot(ch["sc"][:, hp * CHUNK:(hp + 2) * CHUNK], _pair_blocks(ch["vb"], hp, HGRN_DK))


def _hgrn_out(ch, y_ref, r0, hnorm_ref):
    del ch["sc"], ch["vb"]
    o = jnp.concatenate([ch.pop(("o", hp)) + ch.pop(("o_inter", hp)) for hp in range(0, HGRN_HEADS, 2)], axis=1)
    gate = ch.pop("gate")
    for h in range(HGRN_HEADS):
        hs = slice(h * HGRN_DK, (h + 1) * HGRN_DK)
        oh = o[:, hs]
        on = oh * lax.rsqrt(jnp.mean(oh * oh, axis=-1, keepdims=True) + EPS) * hnorm_ref[:, hs]
        y_ref[r0:r0 + CHUNK, POOL_WIDTH + h * HGRN_DK:POOL_WIDTH + (h + 1) * HGRN_DK] = (
            on * gate[:, hs]).astype(BF)


def _mixer_kernel(xp_ref, xn_ref, nmix_ref, win_ref, poolw_ref, pscale_ref, lbt_ref, hnorm_ref, wout_ref,
                  o_ref, ua_ref, ub_ref, ya_ref, yb_ref, st_ref, hist_ref, *, layer, tiles_per_seq):
    T = xn_ref.shape[0]
    cpt = T // CHUNK
    n_chunks = 2 * cpt
    g = pl.program_id(0)

    def project_block(xn, u_ref, blk):
        cols = slice(blk * MXU_COLS, (blk + 1) * MXU_COLS)
        u_ref[:, cols] = _dot(xn, win_ref[:, cols])

    def out_block(rows, y_ref, blk):
        cols = slice(blk * MXU_COLS, (blk + 1) * MXU_COLS)
        o_ref[rows, cols] = xp_ref[rows, cols] + _dot(y_ref[...], wout_ref[:, cols])

    n_in_blocks = IN_COLS // MXU_COLS
    n_out_blocks = D_MODEL // MXU_COLS

    @pl.when(g == 0)
    def _():
        xn0 = _rms(xp_ref[0:T, :], nmix_ref[...]).astype(BF)
        for blk in range(n_in_blocks):
            project_block(xn0, ua_ref, blk)
        st_ref[...] = jnp.zeros_like(st_ref)
        hist_ref[...] = jnp.zeros_like(hist_ref)

    th = lbt_ref[...]
    ex = jnp.exp(th - jnp.max(th, axis=0, keepdims=True))
    p = ex / jnp.sum(ex, axis=0, keepdims=True)
    lb = jnp.sum(p[0:layer + 1, :], axis=0, keepdims=True)
    f_mid = 0.5 * (1.0 + lb)
    f_amp = 0.5 * (1.0 - lb)
    row = lax.broadcasted_iota(jnp.int32, (CHUNK, HGRN_HEADS * CHUNK), 0)
    col = lax.broadcasted_iota(jnp.int32, (CHUNK, HGRN_HEADS * CHUNK), 1)
    causal = row >= (col & (CHUNK - 1))
    tri2 = jnp.where(causal[:, :2 * CHUNK], 1.0, 0.0).astype(BF)

    j0 = lax.rem(2 * g, tiles_per_seq)
    st = jnp.where(j0 == 0, 0.0, st_ref[...])
    hist = jnp.where(j0 == 0, 0.0, hist_ref[...])

    def chunk_src(c):
        return (ua_ref, ya_ref, c * CHUNK) if c < cpt else (ub_ref, yb_ref, (c - cpt) * CHUNK)

    early = list(range(-1, cpt - 1))
    late = list(range(cpt - 1, n_chunks + 2))
    proj_b_slots = {s: [] for s in early}
    for blk in range(n_in_blocks):
        proj_b_slots[early[max(0, blk - 1) * len(early) // (n_in_blocks - 1)]].append(blk)
    proj_a_slots = {s: [] for s in late}
    out_a_slots = {s: [] for s in late}
    proj_a_todo = list(range(n_in_blocks))
    out_a_todo = list(range(n_out_blocks))
    n_late_jobs = n_in_blocks + n_out_blocks
    for i, s in enumerate(late):
        for k in range((i + 1) * n_late_jobs // len(late) - i * n_late_jobs // len(late)):
            if out_a_todo and s >= cpt + 2 and (k == 0 or not proj_a_todo):
                out_a_slots[s].append(out_a_todo.pop(0))
            else:
                proj_a_slots[s].append(proj_a_todo.pop(0))
    assert not out_a_todo and not proj_a_todo

    xn_b = _rms(xp_ref[T:2 * T, :], nmix_ref[...]).astype(BF)
    xn_a = None
    hist_b = None
    pool_b_slot0 = cpt // 2
    assert len(POOL_WINDOWS) <= cpt and pool_b_slot0 + len(POOL_WINDOWS) <= n_chunks + 2
    st_ops = _state_operands(st)
    chunks = [None] * n_chunks
    for slot in range(-1, n_chunks + 2):
        big = [functools.partial(project_block, xn_b, ub_ref, blk) for blk in proj_b_slots.get(slot, ())]
        big += [functools.partial(project_block, xn_a, ua_ref, blk) for blk in proj_a_slots.get(slot, ())]
        big += [functools.partial(out_block, slice(0, T), ya_ref, blk) for blk in out_a_slots.get(slot, ())]
        head_pairs = list(range(0, HGRN_HEADS, 2))
        small = [[] for _ in range(1 + len(head_pairs))]
        if 0 <= slot - 1 < n_chunks:
            small[0].append(functools.partial(_hgrn_scores, chunks[slot - 1]))
            for i, hp in enumerate(head_pairs):
                small[1 + i] += [functools.partial(_hgrn_inter, chunks[slot - 1], st_ops, hp),
                                 functools.partial(_hgrn_state_update, chunks[slot - 1], hp)]
        if 0 <= slot < n_chunks:
            small[0].append(functools.partial(_hgrn_cumsum, chunks[slot], tri2))
        if 0 <= slot - 2 < n_chunks:
            for i, hp in enumerate(head_pairs):
                small[1 + i].append(functools.partial(_hgrn_intra, chunks[slot - 2], hp))
        for job in small.pop(0):
            job()
        while big or small:
            if big:
                big.pop(0)()
            if small:
                for job in small.pop(0):
                    job()
        if slot == -1:
            hist_b = ua_ref[T - POOL_HIST:T, 0:POOL_WIDTH]
        if 0 <= slot + 1 < len(POOL_WINDOWS):
            _pool_group(ua_ref, ya_ref, hist, j0 * T, poolw_ref, pscale_ref, slot + 1)
        if 0 <= slot - pool_b_slot0 < len(POOL_WINDOWS):
            _pool_group(ub_ref, yb_ref, hist_b, (j0 + 1) * T, poolw_ref, pscale_ref, slot - pool_b_slot0)
        if slot == cpt - 2:
            xn_a = _rms(xn_ref[...], nmix_ref[...]).astype(BF)
        if 0 <= slot + 1 < n_chunks:
            u_ref, _, r0 = chunk_src(slot + 1)
            chunks[slot + 1] = _hgrn_gates(u_ref, r0, f_mid, f_amp)
        if 0 <= slot < n_chunks:
            _hgrn_decays(chunks[slot])
        if 0 <= slot - 1 < n_chunks:
            st = _hgrn_mask_and_state(chunks[slot - 1], st, causal)
            if slot < n_chunks:
                st_ops = _state_operands(st)
        if 0 <= slot - 2 < n_chunks:
            _, y_ref, r0 = chunk_src(slot - 2)
            _hgrn_out(chunks[slot - 2], y_ref, r0, hnorm_ref)

    for blk in range(n_out_blocks):
        out_block(slice(T, 2 * T), yb_ref, blk)
    st_ref[...] = st
    hist_ref[...] = ub_ref[T - POOL_HIST:T, 0:POOL_WIDTH]


def _mixer_call(x, norm_mix, w_in, pool_w, pool_scale, lb_theta, hgrn_norm, w_out, layer):
    B, S, D = x.shape
    T = MIX_TILE
    tiles_per_seq = S // T
    assert S % (2 * T) == 0
    n_tiles = B * tiles_per_seq
    x2 = x.reshape(B * S, D)
    out = pl.pallas_call(
        functools.partial(_mixer_kernel, layer=layer, tiles_per_seq=tiles_per_seq),
        grid=(n_tiles // 2,),
        in_specs=[
            pl.BlockSpec((2 * T, D), lambda g: (g, 0)),
            pl.BlockSpec((T, D), lambda g: (jnp.minimum(2 * g + 2, n_tiles - 1), 0)),
            _const_spec((1, D)),
            _const_spec((D, IN_COLS)),
            _const_spec(pool_w.shape),
            _const_spec((1, POOL_WIDTH)),
            _const_spec(lb_theta.shape),
            _const_spec((1, HGRN_W)),
            _const_spec((D, D)),
        ],
        out_specs=pl.BlockSpec((2 * T, D), lambda g: (g, 0)),
        out_shape=jax.ShapeDtypeStruct((B * S, D), F32),
        scratch_shapes=[
            pltpu.VMEM((T, IN_COLS), F32),
            pltpu.VMEM((T, IN_COLS), F32),
            pltpu.VMEM((T, D), BF),
            pltpu.VMEM((T, D), BF),
            pltpu.VMEM((HGRN_DK, HGRN_W), F32),
            pltpu.VMEM((POOL_HIST, POOL_WIDTH), F32),
        ],
        compiler_params=pltpu.CompilerParams(
            dimension_semantics=("arbitrary",), vmem_limit_bytes=VMEM_LIMIT),
        name="token_mixer",
    )(x2, x2, norm_mix, w_in, pool_w, pool_scale, lb_theta, hgrn_norm, w_out)
    return out.reshape(B, S, D)


def _attn_mlp_kernel(h_ref, qk_ref, vw_ref, nxq_ref, nmlp_ref, wup_ref, wdown_ref,
                     nfin_ref, o_ref, h2_ref, hn2_ref, att_ref, *, final_norm):
    s = pl.program_id(0)

    @pl.when(s == 0)
    def _():
        h2_ref[...] = jnp.zeros_like(h2_ref)
        hn2_ref[...] = jnp.zeros_like(hn2_ref)

    n_blk = D_FF // FF_CHUNK
    assert n_blk >= XATTN_HEADS + 2
    T = h_ref.shape[0]
    M = qk_ref.shape[2] // XATTN_HEADS
    h = h_ref[...]
    att = {}

    def head_cols(hd):
        return slice(hd * M, (hd + 1) * M)

    def attn_matmuls(i):
        if i == 1:
            att["s"] = _dot(att.pop("hn"), qk_ref[0])
        if i == n_blk - 1:
            for r in range(0, T, T // 2):
                att["proj", r] = _dot(att_ref[r:r + T // 2, :], vw_ref[0])

    def attn_vectors(i):
        if i == 0:
            att["hn"] = _rms(h, nxq_ref[...]).astype(BF)
        if 1 <= i <= XATTN_HEADS:
            sc = att["s"][:, head_cols(i - 1)]
            p = jnp.exp(sc - jnp.max(sc, axis=-1, keepdims=True))
            att_ref[:, head_cols(i - 1)] = (p * (1.0 / jnp.sum(p, axis=-1, keepdims=True))).astype(BF)
        if i == n_blk - 1:
            for r in range(0, T, T // 2):
                rows = slice(r, r + T // 2)
                h2 = h_ref[rows, :] + att.pop(("proj", r))
                h2_ref[rows, :] = h2
                hn2_ref[rows, :] = _rms(h2, nmlp_ref[...]).astype(BF)

    acc = None
    up = act = None
    for i in range(n_blk):
        if i >= 1:
            down = _dot(act, wdown_ref[(i - 1) * FF_CHUNK:i * FF_CHUNK, :])
            acc = h2_ref[...] + down if i == 1 else acc + down
        attn_matmuls(i)
        up = _dot(hn2_ref[...], wup_ref[:, i * FF_CHUNK:(i + 1) * FF_CHUNK])
        a = jnp.maximum(up, 0.0)
        act = (a * a).astype(BF)
        attn_vectors(i)
    halves = [slice(r, r + T // 2) for r in range(0, T, T // 2)]
    downs = [_dot(act[rows, :], wdown_ref[(n_blk - 1) * FF_CHUNK:, :]) for rows in halves]
    for rows, down in zip(halves, downs):
        out = acc[rows, :] + down
        o_ref[rows, :] = _rms(out, nfin_ref[...]) if final_norm else out


def _attn_mlp_call(h, qk, vw, norm_xq, norm_mlp, w_up, w_down, norm_final, final_norm):
    B, S, D = h.shape
    HM = vw.shape[1]
    T = FFN_TILE
    tiles_per_seq = S // T
    n_tiles = B * tiles_per_seq

    def attn_tile(s):
        return jnp.minimum(s, n_tiles - 1)

    out = pl.pallas_call(
        functools.partial(_attn_mlp_kernel, final_norm=final_norm),
        grid=(n_tiles + 1,),
        in_specs=[
            pl.BlockSpec((T, D), lambda s: (attn_tile(s), 0)),
            pl.BlockSpec((1, D, HM), lambda s: (attn_tile(s) // tiles_per_seq, 0, 0)),
            pl.BlockSpec((1, HM, D), lambda s: (attn_tile(s) // tiles_per_seq, 0, 0)),
            _const_spec((1, D)),
            _const_spec((1, D)),
            _const_spec((D, D_FF)),
            _const_spec((D_FF, D)),
            _const_spec((1, D)),
        ],
        out_specs=pl.BlockSpec((T, D), lambda s: (jnp.maximum(s - 1, 0), 0)),
        out_shape=jax.ShapeDtypeStruct((B * S, D), F32),
        scratch_shapes=[
            pltpu.VMEM((T, D), F32),
            pltpu.VMEM((T, D), BF),
            pltpu.VMEM((T, HM), BF),
        ],
        compiler_params=pltpu.CompilerParams(
            dimension_semantics=("arbitrary",), vmem_limit_bytes=VMEM_LIMIT),
        name="xattn_mlp",
    )(h.reshape(B * S, D), qk, vw, norm_xq, norm_mlp, w_up, w_down, norm_final)
    return out.reshape(B, S, D)


def kernel(x, mem, norm_mix, w_in, pool_w, pool_scale, lb_theta, hgrn_norm, w_out, norm_xq, norm_mem,
           xw_q, xw_kv, xw_o, norm_mlp, w_up, w_down, norm_final):
    depth = norm_mix.shape[0]
    h = x
    for l in range(depth):
        qk, vw = _kv_call(mem, norm_mem[l][None], xw_kv[l].astype(BF), xw_q[l].astype(BF), xw_o[l].astype(BF))
        h = _mixer_call(h, norm_mix[l][None], w_in[l].astype(BF), pool_w[l].astype(BF),
                        pool_scale[l][None], lb_theta, hgrn_norm[l][None], w_out[l].astype(BF), l)
        h = _attn_mlp_call(h, qk, vw, norm_xq[l][None],
                           norm_mlp[l][None], w_up[l].astype(BF), w_down[l].astype(BF),
                           norm_final[None], final_norm=(l == depth - 1))
    return h
```

```python
import functools
import math

import jax
import jax.numpy as jnp
from jax import lax
from jax.experimental import pallas as pl
from jax.experimental.pallas import tpu as pltpu

D_MODEL = 1024
CHUNK = 64
POOL_WIDTH = 512
POOL_WINDOWS = (2, 4, 8, 16)
POOL_GW = POOL_WIDTH // len(POOL_WINDOWS)
POOL_HIST = 16
HGRN_HEADS = 4
HGRN_DK = 128
HGRN_W = HGRN_HEADS * HGRN_DK
IN_COLS = POOL_WIDTH + 4 * HGRN_W
XATTN_HEADS = 4
XATTN_HD = D_MODEL // XATTN_HEADS
D_FF = 4 * D_MODEL
EPS = 1e-6

MXU_COLS = 256
MIX_TILE = 512
FFN_TILE = 512
FF_CHUNK = 512
X_AHEAD = 2
X_RING = 2 * X_AHEAD + 4
VMEM_LIMIT = 56 * 1024 * 1024

BF = jnp.bfloat16
F32 = jnp.float32


def _dot(a, b):
    return jnp.dot(a, b, preferred_element_type=F32)


def _dot_nt(a, b):
    return lax.dot_general(a, b, (((1,), (1,)), ((), ())), preferred_element_type=F32)


def _dot_tn(a, b):
    return lax.dot_general(a, b, (((0,), (0,)), ((), ())), preferred_element_type=F32)


def _rms(x, gain_row):
    ms = jnp.mean(x * x, axis=-1, keepdims=True)
    return x * lax.rsqrt(ms + EPS) * gain_row


def _const_spec(shape):
    return pl.BlockSpec(shape, lambda *_: (0,) * len(shape), pipeline_mode=pl.Buffered(1))


def _kv_kernel(mem_ref, nmem_ref, wkv_ref, wq_ref, wo_ref, qk_ref, vw_ref):
    M = mem_ref.shape[1]
    mn = _rms(mem_ref[0], nmem_ref[...]).astype(BF)
    kb = (_dot(mn, wkv_ref[:, :D_MODEL]) * (1.0 / math.sqrt(XATTN_HD))).astype(BF)
    vb = _dot(mn, wkv_ref[:, D_MODEL:]).astype(BF)
    for hd in range(XATTN_HEADS):
        cols = slice(hd * XATTN_HD, (hd + 1) * XATTN_HD)
        qk_ref[0, :, hd * M:(hd + 1) * M] = _dot_nt(wq_ref[:, cols], kb[:, cols]).astype(BF)
        vw_ref[0, hd * M:(hd + 1) * M, :] = _dot(vb[:, cols], wo_ref[cols, :]).astype(BF)


def _kv_call(mem, norm_mem, wkv, wq, wo):
    B, M, D = mem.shape
    return pl.pallas_call(
        _kv_kernel,
        grid=(B,),
        in_specs=[
            pl.BlockSpec((1, M, D), lambda b: (b, 0, 0)),
            _const_spec((1, D)),
            _const_spec((D, 2 * D)),
            _const_spec((D, D)),
            _const_spec((D, D)),
        ],
        out_specs=[
            pl.BlockSpec((1, D, XATTN_HEADS * M), lambda b: (b, 0, 0)),
            pl.BlockSpec((1, XATTN_HEADS * M, D), lambda b: (b, 0, 0)),
        ],
        out_shape=[jax.ShapeDtypeStruct((B, D, XATTN_HEADS * M), BF),
                   jax.ShapeDtypeStruct((B, XATTN_HEADS * M, D), BF)],
        compiler_params=pltpu.CompilerParams(
            dimension_semantics=("arbitrary",), vmem_limit_bytes=VMEM_LIMIT),
        name="xattn_kv",
    )(mem, norm_mem, wkv, wq, wo)


def _head_blocks(a, width):
    zero = jnp.zeros((CHUNK, width), a.dtype)
    return jnp.concatenate([
        jnp.concatenate([a[:, h * width:(h + 1) * width] if h == r else zero
                         for h in range(HGRN_HEADS)], axis=1)
        for r in range(HGRN_HEADS)], axis=0)


def _pool_group(u_ref, y_ref, hist, pos0, poolw_ref, pscale_ref, g):
    T = u_ref.shape[0]
    w = POOL_WINDOWS[g]
    t_pos = pos0 + lax.broadcasted_iota(jnp.int32, (T, POOL_GW), 0)
    cols = slice(g * POOL_GW, (g + 1) * POOL_GW)
    e = jnp.concatenate([hist[:, cols], u_ref[:, cols]], axis=0)
    s = e
    sh = 1
    while sh < w:
        s = s + pltpu.roll(s, sh, axis=0)
        sh *= 2
    cnt = jnp.minimum(t_pos + 1, w).astype(F32)
    yp = s[POOL_HIST:, :] / cnt - e[POOL_HIST:, :]
    yg = _dot(yp.astype(BF), poolw_ref[g]) * pscale_ref[:, cols]
    y_ref[:, cols] = yg.astype(BF)


def _hgrn_gates(u_ref, r0, f_mid, f_amp):
    rows = slice(r0, r0 + CHUNK)
    zq = u_ref[rows, POOL_WIDTH:POOL_WIDTH + HGRN_W]
    zf = u_ref[rows, POOL_WIDTH + HGRN_W:POOL_WIDTH + 2 * HGRN_W]
    zi = u_ref[rows, POOL_WIDTH + 2 * HGRN_W:POOL_WIDTH + 3 * HGRN_W]
    zg = u_ref[rows, POOL_WIDTH + 3 * HGRN_W:POOL_WIDTH + 4 * HGRN_W]
    f = f_mid + f_amp * jnp.tanh(0.5 * zf)
    l2f = jnp.log2(f)
    hq = 0.5 * zq
    hg = 0.5 * zg
    hi = l2f.astype(BF)
    lo = (l2f - hi.astype(F32)).astype(BF)
    return {"kk": 1.0 - f, "q": hq + hq * jnp.tanh(hq), "gate": hg + hg * jnp.tanh(hg),
            "vb": zi.astype(BF), "l2f_split": jnp.concatenate([hi, lo], axis=0)}


def _hgrn_cumsum(ch, tri2):
    ch["G"] = _dot(tri2, ch.pop("l2f_split"))


def _hgrn_decays(ch):
    G = ch.pop("G")
    g_mid = G[CHUNK // 2 - 1:CHUNK // 2, :]
    g_last = G[CHUNK - 1:CHUNK, :]
    e_fwd = jnp.exp2(G - g_mid)
    q_rel = ch.pop("q") * e_fwd
    k_rel = ch.pop("kk") * (1.0 / e_fwd)
    k_end = (k_rel * jnp.exp2(g_last - g_mid)).astype(BF)
    vb = ch.pop("vb")
    ch["q_dec"] = (q_rel * jnp.exp2(g_mid)).astype(BF)
    ch["q_rel"] = q_rel.astype(BF)
    ch["k_rel_blocks"] = _head_blocks(k_rel, HGRN_DK).T.astype(BF)
    ch["k_end"] = k_end
    ch["vb"] = vb
    ch["decay"] = jnp.exp2(g_last)


def _pair_blocks(a, hp, width):
    zero = jnp.zeros((a.shape[0], width), a.dtype)
    a0 = a[:, hp * width:(hp + 1) * width]
    a1 = a[:, (hp + 1) * width:(hp + 2) * width]
    return jnp.concatenate([jnp.concatenate([a0, zero], axis=1),
                            jnp.concatenate([zero, a1], axis=1)], axis=0)


def _hgrn_scores(ch):
    ch["sc"] = _dot(ch.pop("q_rel"), ch.pop("k_rel_blocks"))


def _state_operands(st):
    return {hp: _pair_blocks(st, hp, HGRN_DK).T.astype(BF) for hp in range(0, HGRN_HEADS, 2)}


def _hgrn_inter(ch, st_ops, hp):
    pair = slice(hp * HGRN_DK, (hp + 2) * HGRN_DK)
    ch["o_inter", hp] = _dot(ch["q_dec"][:, pair], st_ops[hp])


def _hgrn_state_update(ch, hp):
    vb = ch["vb"]
    v_pair = jnp.concatenate([vb[:, hp * HGRN_DK:(hp + 1) * HGRN_DK],
                              vb[:, (hp + 1) * HGRN_DK:(hp + 2) * HGRN_DK]], axis=0)
    ch["d_st", hp] = _dot_tn(v_pair, _pair_blocks(ch["k_end"], hp, HGRN_DK))


def _hgrn_mask_and_state(ch, st, causal):
    ch["sc"] = jnp.where(causal, ch["sc"], 0.0).astype(BF)
    del ch["q_dec"], ch["k_end"]
    d_st = jnp.concatenate([ch.pop(("d_st", hp)) for hp in range(0, HGRN_HEADS, 2)], axis=1)
    return st * ch.pop("decay") + d_st


def _hgrn_intra(ch, hp):
    ch["o", hp] = _dot(ch["sc"][:, hp * CHUNK:(hp + 2) * CHUNK], _pair_blocks(ch["vb"], hp, HGRN_DK))


def _hgrn_out(ch, y_ref, r0, hnorm_ref):
    del ch["sc"], ch["vb"]
    o = jnp.concatenate([ch.pop(("o", hp)) + ch.pop(("o_inter", hp)) for hp in range(0, HGRN_HEADS, 2)], axis=1)
    gate = ch.pop("gate")
    for h in range(HGRN_HEADS):
        hs = slice(h * HGRN_DK, (h + 1) * HGRN_DK)
        oh = o[:, hs]
        on = oh * lax.rsqrt(jnp.mean(oh * oh, axis=-1, keepdims=True) + EPS) * hnorm_ref[:, hs]
        y_ref[r0:r0 + CHUNK, POOL_WIDTH + h * HGRN_DK:POOL_WIDTH + (h + 1) * HGRN_DK] = (
            on * gate[:, hs]).astype(BF)


def _mixer_kernel(x_hbm, nmix_ref, win_ref, poolw_ref, pscale_ref, lbt_ref, hnorm_ref, wout_ref,
                  o_ref, ua_ref, ub_ref, ya_ref, yb_ref, st_ref, hist_ref, xr_ref, xsem, *,
                  layer, tiles_per_seq, n_tiles):
    T = xr_ref.shape[1]
    cpt = T // CHUNK
    n_chunks = 2 * cpt
    g = pl.program_id(0)

    def x_copy(t):
        slot = t % X_RING
        return pltpu.make_async_copy(x_hbm.at[pl.ds(pl.multiple_of(t * T, T), T), :], xr_ref.at[slot], xsem.at[slot])

    first_wave = 2 * X_AHEAD + 1
    @pl.when(g == 0)
    def _():
        for t in range(first_wave):
            x_copy(t).start()

    for dt in (first_wave, first_wave + 1):
        @pl.when(2 * g + dt < n_tiles)
        def _(dt=dt):
            x_copy(2 * g + dt).start()

    @pl.when(g == 0)
    def _():
        x_copy(0).wait()

    for dt in (1, 2):
        @pl.when(2 * g + dt < n_tiles)
        def _(dt=dt):
            x_copy(2 * g + dt).wait()

    slot_a = (2 * g) % X_RING
    slot_b = (2 * g + 1) % X_RING
    slot_n = (2 * g + 2) % X_RING

    def project_block(xn, u_ref, blk):
        cols = slice(blk * MXU_COLS, (blk + 1) * MXU_COLS)
        u_ref[:, cols] = _dot(xn, win_ref[:, cols])

    def out_block(rows, x_slot, y_ref, blk):
        cols = slice(blk * MXU_COLS, (blk + 1) * MXU_COLS)
        o_ref[rows, cols] = xr_ref[x_slot, :, cols] + _dot(y_ref[...], wout_ref[:, cols])

    n_in_blocks = IN_COLS // MXU_COLS
    n_out_blocks = D_MODEL // MXU_COLS

    @pl.when(g == 0)
    def _():
        xn0 = _rms(xr_ref[0], nmix_ref[...]).astype(BF)
        for blk in range(n_in_blocks):
            project_block(xn0, ua_ref, blk)
        st_ref[...] = jnp.zeros_like(st_ref)
        hist_ref[...] = jnp.zeros_like(hist_ref)

    th = lbt_ref[...]
    ex = jnp.exp(th - jnp.max(th, axis=0, keepdims=True))
    p = ex / jnp.sum(ex, axis=0, keepdims=True)
    lb = jnp.sum(p[0:layer + 1, :], axis=0, keepdims=True)
    f_mid = 0.5 * (1.0 + lb)
    f_amp = 0.5 * (1.0 - lb)
    row = lax.broadcasted_iota(jnp.int32, (CHUNK, HGRN_HEADS * CHUNK), 0)
    col = lax.broadcasted_iota(jnp.int32, (CHUNK, HGRN_HEADS * CHUNK), 1)
    causal = row >= (col & (CHUNK - 1))
    tri2 = jnp.where(causal[:, :2 * CHUNK], 1.0, 0.0).astype(BF)

    j0 = lax.rem(2 * g, tiles_per_seq)
    st = jnp.where(j0 == 0, 0.0, st_ref[...])
    hist = jnp.where(j0 == 0, 0.0, hist_ref[...])

    def chunk_src(c):
        return (ua_ref, ya_ref, c * CHUNK) if c < cpt else (ub_ref, yb_ref, (c - cpt) * CHUNK)

    early = list(range(-1, cpt - 1))
    late = list(range(cpt - 1, n_chunks + 2))
    proj_b_slots = {s: [] for s in early}
    for blk in range(n_in_blocks):
        proj_b_slots[early[max(0, blk - 1) * len(early) // (n_in_blocks - 1)]].append(blk)
    proj_a_slots = {s: [] for s in late}
    out_a_slots = {s: [] for s in late}
    proj_a_todo = list(range(n_in_blocks))
    out_a_todo = list(range(n_out_blocks))
    n_late_jobs = n_in_blocks + n_out_blocks
    for i, s in enumerate(late):
        for k in range((i + 1) * n_late_jobs // len(late) - i * n_late_jobs // len(late)):
            if out_a_todo and s >= cpt + 2 and (k == 0 or not proj_a_todo):
                out_a_slots[s].append(out_a_todo.pop(0))
            else:
                proj_a_slots[s].append(proj_a_todo.pop(0))
    assert not out_a_todo and not proj_a_todo

    xn_b = _rms(xr_ref[slot_b], nmix_ref[...]).astype(BF)
    xn_a = None
    hist_b = None
    pool_b_slot0 = cpt // 2
    assert len(POOL_WINDOWS) <= cpt and pool_b_slot0 + len(POOL_WINDOWS) <= n_chunks + 2
    st_ops = _state_operands(st)
    chunks = [None] * n_chunks
    for slot in range(-1, n_chunks + 2):
        big = [functools.partial(project_block, xn_b, ub_ref, blk) for blk in proj_b_slots.get(slot, ())]
        big += [functools.partial(project_block, xn_a, ua_ref, blk) for blk in proj_a_slots.get(slot, ())]
        big += [functools.partial(out_block, slice(0, T), slot_a, ya_ref, blk) for blk in out_a_slots.get(slot, ())]
        head_pairs = list(range(0, HGRN_HEADS, 2))
        small = [[] for _ in range(1 + len(head_pairs))]
        if 0 <= slot - 1 < n_chunks:
            small[0].append(functools.partial(_hgrn_scores, chunks[slot - 1]))
            for i, hp in enumerate(head_pairs):
                small[1 + i] += [functools.partial(_hgrn_inter, chunks[slot - 1], st_ops, hp),
                                 functools.partial(_hgrn_state_update, chunks[slot - 1], hp)]
        if 0 <= slot < n_chunks:
            small[0].append(functools.partial(_hgrn_cumsum, chunks[slot], tri2))
        if 0 <= slot - 2 < n_chunks:
            for i, hp in enumerate(head_pairs):
                small[1 + i].append(functools.partial(_hgrn_intra, chunks[slot - 2], hp))
        for job in small.pop(0):
            job()
        while big or small:
            if big:
                big.pop(0)()
            if small:
                for job in small.pop(0):
                    job()
        if slot == -1:
            hist_b = ua_ref[T - POOL_HIST:T, 0:POOL_WIDTH]
        if 0 <= slot + 1 < len(POOL_WINDOWS):
            _pool_group(ua_ref, ya_ref, hist, j0 * T, poolw_ref, pscale_ref, slot + 1)
        if 0 <= slot - pool_b_slot0 < len(POOL_WINDOWS):
            _pool_group(ub_ref, yb_ref, hist_b, (j0 + 1) * T, poolw_ref, pscale_ref, slot - pool_b_slot0)
        if slot == cpt - 2:
            xn_a = _rms(xr_ref[slot_n], nmix_ref[...]).astype(BF)
        if 0 <= slot + 1 < n_chunks:
            u_ref, _, r0 = chunk_src(slot + 1)
            chunks[slot + 1] = _hgrn_gates(u_ref, r0, f_mid, f_amp)
        if 0 <= slot < n_chunks:
            _hgrn_decays(chunks[slot])
        if 0 <= slot - 1 < n_chunks:
            st = _hgrn_mask_and_state(chunks[slot - 1], st, causal)
            if slot < n_chunks:
                st_ops = _state_operands(st)
        if 0 <= slot - 2 < n_chunks:
            _, y_ref, r0 = chunk_src(slot - 2)
            _hgrn_out(chunks[slot - 2], y_ref, r0, hnorm_ref)

    for blk in range(n_out_blocks):
        out_block(slice(T, 2 * T), slot_b, yb_ref, blk)
    st_ref[...] = st
    hist_ref[...] = ub_ref[T - POOL_HIST:T, 0:POOL_WIDTH]


def _mixer_call(x, norm_mix, w_in, pool_w, pool_scale, lb_theta, hgrn_norm, w_out, layer):
    B, S, D = x.shape
    T = MIX_TILE
    tiles_per_seq = S // T
    assert S % (2 * T) == 0
    n_tiles = B * tiles_per_seq
    x2 = x.reshape(B * S, D)
    out = pl.pallas_call(
        functools.partial(_mixer_kernel, layer=layer, tiles_per_seq=tiles_per_seq, n_tiles=n_tiles),
        grid=(n_tiles // 2,),
        in_specs=[
            pl.BlockSpec(memory_space=pl.ANY),
            _const_spec((1, D)),
            _const_spec((D, IN_COLS)),
            _const_spec(pool_w.shape),
            _const_spec((1, POOL_WIDTH)),
            _const_spec(lb_theta.shape),
            _const_spec((1, HGRN_W)),
            _const_spec((D, D)),
        ],
        out_specs=pl.BlockSpec((2 * T, D), lambda g: (g, 0)),
        out_shape=jax.ShapeDtypeStruct((B * S, D), F32),
        scratch_shapes=[
            pltpu.VMEM((T, IN_COLS), F32),
            pltpu.VMEM((T, IN_COLS), F32),
            pltpu.VMEM((T, D), BF),
            pltpu.VMEM((T, D), BF),
            pltpu.VMEM((HGRN_DK, HGRN_W), F32),
            pltpu.VMEM((POOL_HIST, POOL_WIDTH), F32),
            pltpu.VMEM((X_RING, T, D), F32),
            pltpu.SemaphoreType.DMA((X_RING,)),
        ],
        compiler_params=pltpu.CompilerParams(
            dimension_semantics=("arbitrary",), vmem_limit_bytes=VMEM_LIMIT),
        name="token_mixer",
    )(x2, norm_mix, w_in, pool_w, pool_scale, lb_theta, hgrn_norm, w_out)
    return out.reshape(B, S, D)


def _attn_mlp_kernel(h_ref, qk_ref, vw_ref, nxq_ref, nmlp_ref, wup_ref, wdown_ref,
                     nfin_ref, o_ref, h2_ref, hn2_ref, att_ref, *, final_norm):
    s = pl.program_id(0)

    @pl.when(s == 0)
    def _():
        h2_ref[...] = jnp.zeros_like(h2_ref)
        hn2_ref[...] = jnp.zeros_like(hn2_ref)

    n_blk = D_FF // FF_CHUNK
    assert n_blk >= XATTN_HEADS + 2
    T = h_ref.shape[0]
    M = qk_ref.shape[2] // XATTN_HEADS
    h = h_ref[...]
    att = {}

    def head_cols(hd):
        return slice(hd * M, (hd + 1) * M)

    def attn_matmuls(i):
        if i == 1:
            att["s"] = _dot(att.pop("hn"), qk_ref[0])
        if i == n_blk - 1:
            for r in range(0, T, T // 2):
                att["proj", r] = _dot(att_ref[r:r + T // 2, :], vw_ref[0])

    def attn_vectors(i):
        if i == 0:
            att["hn"] = _rms(h, nxq_ref[...]).astype(BF)
        if 1 <= i <= XATTN_HEADS:
            sc = att["s"][:, head_cols(i - 1)]
            p = jnp.exp(sc - jnp.max(sc, axis=-1, keepdims=True))
            att_ref[:, head_cols(i - 1)] = (p * (1.0 / jnp.sum(p, axis=-1, keepdims=True))).astype(BF)
        if i == n_blk - 1:
            for r in range(0, T, T // 2):
                rows = slice(r, r + T // 2)
                h2 = h_ref[rows, :] + att.pop(("proj", r))
                h2_ref[rows, :] = h2
                hn2_ref[rows, :] = _rms(h2, nmlp_ref[...]).astype(BF)

    acc = None
    up = act = None
    for i in range(n_blk):
        if i >= 1:
            down = _dot(act, wdown_ref[(i - 1) * FF_CHUNK:i * FF_CHUNK, :])
            acc = h2_ref[...] + down if i == 1 else acc + down
        attn_matmuls(i)
        up = _dot(hn2_ref[...], wup_ref[:, i * FF_CHUNK:(i + 1) * FF_CHUNK])
        a = jnp.maximum(up, 0.0)
        act = (a * a).astype(BF)
        attn_vectors(i)
    halves = [slice(r, r + T // 2) for r in range(0, T, T // 2)]
    downs = [_dot(act[rows, :], wdown_ref[(n_blk - 1) * FF_CHUNK:, :]) for rows in halves]
    for rows, down in zip(halves, downs):
        out = acc[rows, :] + down
        o_ref[rows, :] = _rms(out, nfin_ref[...]) if final_norm else out


def _attn_mlp_call(h, qk, vw, norm_xq, norm_mlp, w_up, w_down, norm_final, final_norm):
    B, S, D = h.shape
    HM = vw.shape[1]
    T = FFN_TILE
    tiles_per_seq = S // T
    n_tiles = B * tiles_per_seq

    def attn_tile(s):
        return jnp.minimum(s, n_tiles - 1)

    out = pl.pallas_call(
        functools.partial(_attn_mlp_kernel, final_norm=final_norm),
        grid=(n_tiles + 1,),
        in_specs=[
            pl.BlockSpec((T, D), lambda s: (attn_tile(s), 0)),
            pl.BlockSpec((1, D, HM), lambda s: (attn_tile(s) // tiles_per_seq, 0, 0)),
            pl.BlockSpec((1, HM, D), lambda s: (attn_tile(s) // tiles_per_seq, 0, 0)),
            _const_spec((1, D)),
            _const_spec((1, D)),
            _const_spec((D, D_FF)),
            _const_spec((D_FF, D)),
            _const_spec((1, D)),
        ],
        out_specs=pl.BlockSpec((T, D), lambda s: (jnp.maximum(s - 1, 0), 0)),
        out_shape=jax.ShapeDtypeStruct((B * S, D), F32),
        scratch_shapes=[
            pltpu.VMEM((T, D), F32),
            pltpu.VMEM((T, D), BF),
            pltpu.VMEM((T, HM), BF),
        ],
        compiler_params=pltpu.CompilerParams(
            dimension_semantics=("arbitrary",), vmem_limit_bytes=VMEM_LIMIT),
        name="xattn_mlp",
    )(h.reshape(B * S, D), qk, vw, norm_xq, norm_mlp, w_up, w_down, norm_final)
    return out.reshape(B, S, D)


def kernel(x, mem, norm_mix, w_in, pool_w, pool_scale, lb_theta, hgrn_norm, w_out, norm_xq, norm_mem,
           xw_q, xw_kv, xw_o, norm_mlp, w_up, w_down, norm_final):
    depth = norm_mix.shape[0]
    h = x
    for l in range(depth):
        qk, vw = _kv_call(mem, norm_mem[l][None], xw_kv[l].astype(BF), xw_q[l].astype(BF), xw_o[l].astype(BF))
        h = _mixer_call(h, norm_mix[l][None], w_in[l].astype(BF), pool_w[l].astype(BF),
                        pool_scale[l][None], lb_theta, hgrn_norm[l][None], w_out[l].astype(BF), l)
        h = _attn_mlp_call(h, qk, vw, norm_xq[l][None],
                           norm_mlp[l][None], w_up[l].astype(BF), w_down[l].astype(BF),
                           norm_final[None], final_norm=(l == depth - 1))
    return h
```
